```python
import jax, jax.numpy as jnp
from jax import lax
import numpy as np

D_MODEL = 2048
BATCH = 16
SEQ = 256
DEPTH = 2
DEC_BATCH = 4
DEC_SEQ = 2048
PAST_LEN = 512

GRID_W = 64
POOL_WIDTH = 512
POOL_GROUPS = 4
POOL_GROUP_DIM = POOL_WIDTH // POOL_GROUPS
POOL_WINDOWS = (2, 4, 8, 16)
ATTN_WIDTH = 512
N_HEADS = 4
V_DIM = ATTN_WIDTH // N_HEADS
QK_DIM = V_DIM // 2
ROPE_AXIS_DIM = QK_DIM // 2
ROPE_BASE = 10000.0
Q_BLOCK = 128
CONV_CH = 512
CONV_TAPS = 31
FOURIER_WIDTH = 512
FOURIER_HEADS = 4
FOURIER_HEAD_DIM = FOURIER_WIDTH // FOURIER_HEADS

MIX_WIDTH = POOL_WIDTH + ATTN_WIDTH + CONV_CH + FOURIER_WIDTH
IN_COLS = POOL_WIDTH + 3 * ATTN_WIDTH + 2 * CONV_CH + FOURIER_WIDTH
D_FF = ((8 * D_MODEL + 3 * 256 - 1) // (3 * 256)) * 256
EPS = 1e-6

kernel_name = "hybrid_diffusion_prefix_step"


def rmsnorm(x, g):
    xf = x.astype(jnp.float32)
    y = xf * lax.rsqrt(jnp.mean(xf * xf, axis=-1, keepdims=True) + EPS)
    return (y * g.astype(jnp.float32)).astype(x.dtype)


def layernorm(x, g, b):
    xf = x.astype(jnp.float32)
    mu = jnp.mean(xf, axis=-1, keepdims=True)
    xc = xf - mu
    y = xc * lax.rsqrt(jnp.mean(xc * xc, axis=-1, keepdims=True) + EPS)
    return (y * g.astype(jnp.float32) + b.astype(jnp.float32)).astype(x.dtype)


def axial_rope_tables(rows):
    t = jnp.arange(rows * GRID_W)
    row = (t // GRID_W).astype(jnp.float32)
    col = (t % GRID_W).astype(jnp.float32)
    inv = 1.0 / (ROPE_BASE ** (jnp.arange(0, ROPE_AXIS_DIM, 2, dtype=jnp.float32) / ROPE_AXIS_DIM))
    ar = row[:, None] * inv[None, :]
    ac = col[:, None] * inv[None, :]
    return (jnp.cos(ar), jnp.sin(ar), jnp.cos(ac), jnp.sin(ac))


def _rope_axis(x, cos, sin):
    half = ROPE_AXIS_DIM // 2
    cos = cos[None, :, None, None, :]
    sin = sin[None, :, None, None, :]
    x1, x2 = x[..., :half], x[..., half:]
    return jnp.concatenate([x1 * cos - x2 * sin, x2 * cos + x1 * sin], axis=-1)


def axial_rope(x, tabs):
    cr, sr, cc, sc = tabs
    xf = x.astype(jnp.float32)
    y = jnp.concatenate([_rope_axis(xf[..., :ROPE_AXIS_DIM], cr, sr),
                         _rope_axis(xf[..., ROPE_AXIS_DIM:], cc, sc)], axis=-1)
    return y.astype(x.dtype)


def multiscale_pool(u, pool_w, pool_scale):
    B, L, _ = u.shape
    uf = u.astype(jnp.float32)
    cs = jnp.concatenate([jnp.zeros((B, 1, POOL_WIDTH), jnp.float32), jnp.cumsum(uf, axis=1)], axis=1)
    t = jnp.arange(L)
    outs = []
    for g, w in enumerate(POOL_WINDOWS):
        sl = slice(g * POOL_GROUP_DIM, (g + 1) * POOL_GROUP_DIM)
        lo = jnp.maximum(t - w // 2, 0)
        hi = jnp.minimum(t + w // 2 - 1, L - 1)
        csg = cs[..., sl]
        s = jnp.take(csg, hi + 1, axis=1) - jnp.take(csg, lo, axis=1)
        cnt = (hi - lo + 1).astype(jnp.float32)[None, :, None]
        outs.append(s / cnt - uf[..., sl])
    p = jnp.stack(outs, axis=2).astype(u.dtype)
    y = jnp.einsum('blgc,gcd->blgd', p, pool_w).reshape(B, L, POOL_WIDTH)
    return y * pool_scale


def conformer_conv(u, dw, dw_b, ln_g, ln_b, pw, pw_b):
    a, b = jnp.split(u, 2, axis=-1)
    g = a * jax.nn.sigmoid(b)
    y = lax.conv_general_dilated(g, dw[:, None, :].astype(g.dtype), (1,),
                                 [(CONV_TAPS // 2, CONV_TAPS // 2)],
                                 dimension_numbers=('NWC', 'WIO', 'NWC'),
                                 feature_group_count=CONV_CH) + dw_b
    y = jax.nn.silu(layernorm(y, ln_g, ln_b))
    return y @ pw + pw_b


def fourier_mix(u, w):
    B, L, _ = u.shape
    uh = u.astype(jnp.float32).reshape(B, L, FOURIER_HEADS, FOURIER_HEAD_DIM)
    f = jnp.fft.fftn(uh, axes=(1, 3), norm='ortho').real
    return f.reshape(B, L, FOURIER_WIDTH).astype(u.dtype) @ w


def diff_attention(q, k, v, lam):
    B, Lq = q.shape[0], q.shape[1]
    nb = Lq // Q_BLOCK
    qb = jnp.moveaxis(q.reshape(B, nb, Q_BLOCK, N_HEADS, 2, QK_DIM), 1, 0)
    scale = QK_DIM ** -0.5

    def block(qi):
        s = jnp.einsum('bqhmd,bkhmd->bhmqk', qi, k, preferred_element_type=jnp.float32) * scale
        p = jax.nn.softmax(s, axis=-1)
        a = p[:, :, 0] - lam * p[:, :, 1]
        return jnp.einsum('bhqk,bkhd->bqhd', a.astype(v.dtype), v)

    o = lax.map(block, qb)
    return jnp.moveaxis(o, 0, 1).reshape(B, Lq, N_HEADS, V_DIM)


def token_mixer(h, lp, lam_init, rope, ctx_k, ctx_v):
    B, L, _ = h.shape
    z = h @ lp['w_in']
    o1 = POOL_WIDTH
    o2 = o1 + ATTN_WIDTH
    o3 = o2 + ATTN_WIDTH
    o4 = o3 + ATTN_WIDTH
    o5 = o4 + 2 * CONV_CH
    u_pool = z[..., :o1]
    q = rmsnorm(z[..., o1:o2].reshape(B, L, N_HEADS, 2, QK_DIM), lp['g_q'])
    k = rmsnorm(z[..., o2:o3].reshape(B, L, N_HEADS, 2, QK_DIM), lp['g_k'])
    v = z[..., o3:o4].reshape(B, L, N_HEADS, V_DIM)
    u_conv = z[..., o4:o5]
    u_four = z[..., o5:]

    if rope is None:
        keys, vals = k, v
    else:
        q = axial_rope(q, rope)
        keys = jnp.concatenate([axial_rope(k, rope), ctx_k.astype(k.dtype)], axis=1)
        vals = jnp.concatenate([v, ctx_v.astype(v.dtype)], axis=1)

    lv = lp['lam'].astype(jnp.float32)
    lam = jnp.exp(jnp.sum(lv[0] * lv[1])) - jnp.exp(jnp.sum(lv[2] * lv[3])) + lam_init
    att = diff_attention(q, keys, vals, lam)
    att = (rmsnorm(att, lp['g_subln']) * (1.0 - lam_init)).reshape(B, L, ATTN_WIDTH)

    y_pool = multiscale_pool(u_pool, lp['pool_w'], lp['pool_scale'])
    y_conv = conformer_conv(u_conv, lp['conv_dw'], lp['conv_dw_b'], lp['conv_ln_g'],
                            lp['conv_ln_b'], lp['conv_pw'], lp['conv_pw_b'])
    y_four = fourier_mix(u_four, lp['fourier_w'])
    y = jnp.concatenate([y_pool, att, y_conv, y_four], axis=-1) @ lp['w_out']
    return y, k, v


def trunk_layer(x, mod, lp, lam_init, rope, ctx_k, ctx_v):
    sh1, sc1, g1, sh2, sc2, g2 = jnp.split(mod, 6, axis=-1)
    h = rmsnorm(x, lp['g_norm1']) * (1.0 + sc1) + sh1
    a, k, v = token_mixer(h, lp, lam_init, rope, ctx_k, ctx_v)
    x = x + g1 * a
    h = rmsnorm(x, lp['g_norm2']) * (1.0 + sc2) + sh2
    f = (jax.nn.silu(h @ lp['w_gate']) * (h @ lp['w_up'])) @ lp['w_down']
    return x + g2 * f, k, v


def setup_inputs(seed: int = 0) -> dict:
    key = jax.random.key(seed)
    ks = jax.random.split(key, 32)
    f32 = jnp.float32

    def nrm(k, shape, scale=1.0):
        return jax.random.normal(k, shape, f32) * scale

    def gain(k, shape):
        return 1.0 + 0.05 * jax.random.normal(k, shape, f32)

    return {
        "x_prompt": nrm(ks[0], (BATCH, SEQ, D_MODEL)),
        "x_sample": nrm(ks[1], (DEC_BATCH, DEC_SEQ, D_MODEL)),
        "cache_k": nrm(ks[2], (DEC_BATCH, DEPTH, PAST_LEN, N_HEADS, 2, QK_DIM)),
        "cache_v": nrm(ks[3], (DEC_BATCH, DEPTH, PAST_LEN, N_HEADS, V_DIM)),
        "c": nrm(ks[4], (DEC_BATCH, D_MODEL)),
        "c_ctx": nrm(ks[5], (D_MODEL,)),
        "w_ada": nrm(ks[6], (DEPTH, D_MODEL, 6 * D_MODEL), 0.5 * D_MODEL ** -0.5),
        "b_ada": nrm(ks[7], (DEPTH, 6 * D_MODEL), 0.02),
        "g_norm1": gain(ks[8], (DEPTH, D_MODEL)),
        "w_in": nrm(ks[9], (DEPTH, D_MODEL, IN_COLS), D_MODEL ** -0.5),
        "pool_w": nrm(ks[10], (DEPTH, POOL_GROUPS, POOL_GROUP_DIM, POOL_GROUP_DIM), POOL_GROUP_DIM ** -0.5),
        "pool_scale": gain(ks[11], (DEPTH, POOL_WIDTH)),
        "g_q": gain(ks[12], (DEPTH, QK_DIM)),
        "g_k": gain(ks[13], (DEPTH, QK_DIM)),
        "lam": nrm(ks[14], (DEPTH, 4, QK_DIM), 0.1),
        "g_subln": gain(ks[15], (DEPTH, V_DIM)),
        "conv_dw": nrm(ks[16], (DEPTH, CONV_TAPS, CONV_CH), CONV_TAPS ** -0.5),
        "conv_dw_b": nrm(ks[17], (DEPTH, CONV_CH), 0.02),
        "conv_ln_g": gain(ks[18], (DEPTH, CONV_CH)),
        "conv_ln_b": nrm(ks[19], (DEPTH, CONV_CH), 0.02),
        "conv_pw": nrm(ks[20], (DEPTH, CONV_CH, CONV_CH), CONV_CH ** -0.5),
        "conv_pw_b": nrm(ks[21], (DEPTH, CONV_CH), 0.02),
        "fourier_w": nrm(ks[22], (DEPTH, FOURIER_WIDTH, FOURIER_WIDTH), FOURIER_WIDTH ** -0.5),
        "w_out": nrm(ks[23], (DEPTH, MIX_WIDTH, D_MODEL), MIX_WIDTH ** -0.5),
        "g_norm2": gain(ks[24], (DEPTH, D_MODEL)),
        "w_gate": nrm(ks[25], (DEPTH, D_MODEL, D_FF), D_MODEL ** -0.5),
        "w_up": nrm(ks[26], (DEPTH, D_MODEL, D_FF), D_MODEL ** -0.5),
        "w_down": nrm(ks[27], (DEPTH, D_FF, D_MODEL), D_FF ** -0.5),
    }


def reference(x_prompt, x_sample, cache_k, cache_v, c, c_ctx, w_ada, b_ada, g_norm1, w_in,
              pool_w, pool_scale, g_q, g_k, lam, g_subln, conv_dw, conv_dw_b, conv_ln_g,
              conv_ln_b, conv_pw, conv_pw_b, fourier_w, w_out, g_norm2, w_gate, w_up, w_down):
    rows = x_sample.shape[1] // GRID_W
    rope = axial_rope_tables(rows)
    yp, ys = x_prompt, x_sample
    new_k, new_v = [], []
    for l in range(DEPTH):
        lp = dict(w_in=w_in[l], pool_w=pool_w[l], pool_scale=pool_scale[l], g_q=g_q[l], g_k=g_k[l],
                  lam=lam[l], g_subln=g_subln[l], conv_dw=conv_dw[l], conv_dw_b=conv_dw_b[l],
                  conv_ln_g=conv_ln_g[l], conv_ln_b=conv_ln_b[l], conv_pw=conv_pw[l],
                  conv_pw_b=conv_pw_b[l], fourier_w=fourier_w[l], w_out=w_out[l],
                  g_norm1=g_norm1[l], g_norm2=g_norm2[l], w_gate=w_gate[l], w_up=w_up[l],
                  w_down=w_down[l])
        lam_init = 0.8 - 0.6 * float(np.exp(-0.3 * l))
        mod_ctx = (jax.nn.silu(c_ctx) @ w_ada[l] + b_ada[l])[None, None, :]
        mod_lat = (jax.nn.silu(c) @ w_ada[l] + b_ada[l])[:, None, :]
        yp, kc, vc = trunk_layer(yp, mod_ctx, lp, lam_init, None, None, None)
        new_k.append(kc)
        new_v.append(vc)
        ys, _, _ = trunk_layer(ys, mod_lat, lp, lam_init, rope, cache_k[:, l], cache_v[:, l])
    new_cache_k = jnp.stack(new_k, axis=1)
    new_cache_v = jnp.stack(new_v, axis=1)
    return (yp, ys, new_cache_k, new_cache_v)
```

```python
import functools

import numpy as np
import jax
import jax.numpy as jnp
from jax import lax
from jax.experimental import pallas as pl
from jax.experimental.pallas import tpu as pltpu

D_MODEL = 2048
DEPTH = 2
GRID_W = 64
MIXER_WIDTH = 512
POOL_GROUP_DIM = 128
POOL_WINDOWS = (2, 4, 8, 16)
N_HEADS = 4
V_DIM = 128
QK_DIM = 64
ROPE_AXIS_DIM = 32
ROPE_BASE = 10000.0
CONV_TAPS = 31
FOURIER_HEAD_DIM = 128
IN_COLS = 7 * MIXER_WIDTH
D_FF = 5632
EPS = 1e-6

LANES = 128
HALO = 16
SEQ_CHUNK = 256
CONV_SUB = 32
VMEM_LIMIT = 56 * 1024 * 1024

BF16 = jnp.bfloat16
F32 = jnp.float32


def _params(*semantics):
    return pltpu.CompilerParams(dimension_semantics=semantics, vmem_limit_bytes=VMEM_LIMIT)


@functools.lru_cache(maxsize=None)
def _rope_tables(seq_len):
    t = np.arange(seq_len)
    pos = np.stack([(t // GRID_W), (t % GRID_W)], axis=0).astype(np.float64)
    half = ROPE_AXIS_DIM // 2
    inv = 1.0 / (ROPE_BASE ** (np.arange(0, ROPE_AXIS_DIM, 2, dtype=np.float64) / ROPE_AXIS_DIM))
    lane = np.arange(LANES)
    d = lane % QK_DIM
    axis = d // ROPE_AXIS_DIM
    freq = d % half
    ang = pos[axis, :].T * inv[freq][None, :]
    first = ((d % ROPE_AXIS_DIM) < half)[None, :]
    cos = np.cos(ang)
    sin = np.sin(ang)
    sin_next = np.where(first, -sin, 0.0)
    sin_prev = np.where(first, 0.0, sin)
    return (cos.astype(np.float32), sin_next.astype(np.float32), sin_prev.astype(np.float32))


@functools.lru_cache(maxsize=None)
def _dft_tables(seq_len):
    def cs(n):
        k = np.arange(n)
        m = (k[:, None] * k[None, :]) % n
        a = 2.0 * np.pi * m.astype(np.float64) / n
        return np.cos(a), np.sin(a)
    cl, sl = cs(seq_len)
    cc, sc = cs(FOURIER_HEAD_DIM)
    pos = np.concatenate([cl, -sl], axis=1).astype(np.float32)
    ch = np.concatenate([cc, sc], axis=1).astype(np.float32)
    return pos, ch


@functools.lru_cache(maxsize=None)
def _group_mean_matrix():
    g = np.arange(MIXER_WIDTH) // QK_DIM
    p = (g[:, None] == g[None, :]).astype(np.float32) / QK_DIM
    return jnp.asarray(p, dtype=BF16)


def _mod_kernel(c_ref, w_ref, b_ref, o_ref):
    c = c_ref[...]
    s = (c * jax.nn.sigmoid(c)).astype(BF16)
    o_ref[...] = jnp.dot(s, w_ref[...].astype(BF16), preferred_element_type=F32) + b_ref[...]


def _modulation(cc, w_ada, b_ada):
    tn = 1024
    n_cols = 6 * D_MODEL
    return pl.pallas_call(
        _mod_kernel,
        grid=(DEPTH, n_cols // tn),
        in_specs=[
            pl.BlockSpec((8, D_MODEL), lambda l, j: (0, 0)),
            pl.BlockSpec((None, D_MODEL, tn), lambda l, j: (l, 0, j)),
            pl.BlockSpec((None, 1, tn), lambda l, j: (l, 0, j)),
        ],
        out_specs=pl.BlockSpec((None, 8, tn), lambda l, j: (l, 0, j)),
        out_shape=jax.ShapeDtypeStruct((DEPTH, 8, n_cols), F32),
        compiler_params=_params("arbitrary", "arbitrary"),
        name="adaln_mod",
    )(cc, w_ada, b_ada.reshape(DEPTH, 1, n_cols))


def _mod_spec(layer, chunk, row_of_block):
    return pl.BlockSpec((None, None, 1, D_MODEL),
                        lambda i, *_: (layer, row_of_block(i), 0, chunk))


def _rms(x, gain):
    return x * lax.rsqrt(jnp.mean(x * x, axis=-1, keepdims=True) + EPS) * gain


def _rope(t, cos, sin_next, sin_prev):
    outs = []
    for c in range(MIXER_WIDTH // LANES):
        tc = t[:, c * LANES:(c + 1) * LANES]
        outs.append(tc * cos
                    + pltpu.roll(tc, LANES - ROPE_AXIS_DIM // 2, axis=1) * sin_next
                    + pltpu.roll(tc, ROPE_AXIS_DIM // 2, axis=1) * sin_prev)
    return jnp.concatenate(outs, axis=1)


def _inproj_kernel(latent, *refs):
    if latent:
        (x_ref, sh_ref, sc_ref, g1_ref, w_ref, gq_ref, gk_ref, p_ref, cos_ref, sn_ref, sp_ref,
         upool_ref, q_ref, k_ref, v_ref, glu_ref, ufour_ref, h_scr, a_scr) = refs
    else:
        (x_ref, sh_ref, sc_ref, g1_ref, w_ref, gq_ref, gk_ref, p_ref,
         upool_ref, q_ref, k_ref, v_ref, glu_ref, ufour_ref, kf_ref, vf_ref, h_scr, a_scr) = refs
    n = pl.program_id(1)

    @pl.when(n == 0)
    def _():
        h = _rms(x_ref[...], g1_ref[...]) * (1.0 + sc_ref[...]) + sh_ref[...]
        h_scr[...] = h.astype(BF16)

    z = jnp.dot(h_scr[...], w_ref[...], preferred_element_type=F32)

    def qk_norm(gain_ref):
        ms = jnp.dot((z * z).astype(BF16), p_ref[...], preferred_element_type=F32)
        return z * lax.rsqrt(ms + EPS) * gain_ref[...]

    @pl.when(n == 0)
    def _():
        upool_ref[...] = z

    @pl.when(n == 1)
    def _():
        qn = qk_norm(gq_ref)
        if latent:
            qn = _rope(qn, cos_ref[...], sn_ref[...], sp_ref[...])
        q_ref[...] = (qn * (QK_DIM ** -0.5)).astype(BF16)

    @pl.when(n == 2)
    def _():
        kn = qk_norm(gk_ref)
        if latent:
            kn = _rope(kn, cos_ref[...], sn_ref[...], sp_ref[...])
        else:
            kf_ref[...] = kn
        k_ref[...] = kn.astype(BF16)

    @pl.when(n == 3)
    def _():
        if not latent:
            vf_ref[...] = z
        v_ref[...] = z.astype(BF16)

    @pl.when(n == 4)
    def _():
        a_scr[...] = z

    @pl.when(n == 5)
    def _():
        glu_ref[...] = a_scr[...] * jax.nn.sigmoid(z)

    @pl.when(n == 6)
    def _():
        ufour_ref[...] = z.astype(BF16)


def _inproj(x, mod, g_norm1, w_in, g_q, g_k, layer, seq_len, latent, tm):
    m_rows = x.shape[0]
    w = MIXER_WIDTH
    if latent:
        row_of = lambda i: 1 + i // (seq_len // tm)
    else:
        row_of = lambda i: 0
    row_spec = lambda: pl.BlockSpec((tm, w), lambda i, n: (i, 0))
    vec_spec = lambda: pl.BlockSpec((1, w), lambda i, n: (0, 0))
    in_specs = [
        pl.BlockSpec((tm, D_MODEL), lambda i, n: (i, 0)),
        _mod_spec(layer, 0, row_of),
        _mod_spec(layer, 1, row_of),
        pl.BlockSpec((None, 1, D_MODEL), lambda i, n: (layer, 0, 0)),
        pl.BlockSpec((None, D_MODEL, w), lambda i, n: (layer, 0, n)),
        vec_spec(), vec_spec(),
        pl.BlockSpec((w, w), lambda i, n: (0, 0)),
    ]
    args = [x, mod, mod, g_norm1.reshape(DEPTH, 1, D_MODEL), w_in,
            jnp.tile(g_q[layer], w // QK_DIM).reshape(1, w),
            jnp.tile(g_k[layer], w // QK_DIM).reshape(1, w),
            _group_mean_matrix()]
    out_shape = [jax.ShapeDtypeStruct((m_rows, w), F32)] + \
                [jax.ShapeDtypeStruct((m_rows, w), BF16)] * 3 + \
                [jax.ShapeDtypeStruct((m_rows, w), F32), jax.ShapeDtypeStruct((m_rows, w), BF16)]
    out_specs = [row_spec() for _ in range(6)]
    if latent:
        blocks_per_seq = seq_len // tm
        for tab in _rope_tables(seq_len):
            in_specs.append(pl.BlockSpec((tm, LANES), lambda i, n: (i % blocks_per_seq, 0)))
            args.append(jnp.asarray(tab))
    else:
        out_shape += [jax.ShapeDtypeStruct((m_rows, w), F32)] * 2
        out_specs += [row_spec(), row_spec()]
    return pl.pallas_call(
        functools.partial(_inproj_kernel, latent),
        grid=(m_rows // tm, IN_COLS // w),
        in_specs=in_specs,
        out_specs=out_specs,
        out_shape=out_shape,
        scratch_shapes=[pltpu.VMEM((tm, D_MODEL), BF16), pltpu.VMEM((tm, w), F32)],
        compiler_params=_params("arbitrary", "arbitrary"),
        name="inproj_lat" if latent else "inproj_ctx",
    )(*args)


def _shift_rows(x, s):
    return pltpu.roll(x, s % x.shape[0], axis=0)


def _seqmix_kernel(chunks_per_seq, up_prev, up_cur, up_next, gl_prev, gl_cur, gl_next,
                   pw_ref, ps_ref, dw_ref, dwb_ref, lng_ref, lnb_ref, cw_ref, cwb_ref,
                   ypool_ref, yconv_ref, pool_scr, conv_scr, acc_scr):
    i = pl.program_id(0)
    r = SEQ_CHUNK
    first = (i % chunks_per_seq) == 0
    last = (i % chunks_per_seq) == chunks_per_seq - 1
    zero_halo = jnp.zeros((HALO, MIXER_WIDTH), F32)

    def fill(scr, prev, cur, nxt):
        scr[0:HALO, :] = jnp.where(first, zero_halo, prev[...])
        scr[HALO:HALO + r, :] = cur[...]
        scr[HALO + r:2 * HALO + r, :] = jnp.where(last, zero_halo, nxt[...])

    fill(pool_scr, up_prev, up_cur, up_next)
    fill(conv_scr, gl_prev, gl_cur, gl_next)

    row = lax.broadcasted_iota(jnp.int32, (r, 1), 0) + (i % chunks_per_seq) * r
    seq_len = chunks_per_seq * r
    outs = []
    for g, win in enumerate(POOL_WINDOWS):
        lanes = slice(g * POOL_GROUP_DIM, (g + 1) * POOL_GROUP_DIM)
        u = pool_scr[:, lanes]
        s = _shift_rows(u, 1) + u
        span = 2
        while span < win:
            s = _shift_rows(s, span // 2) + _shift_rows(s, -(span // 2))
            span *= 2
        lo = jnp.maximum(row - win // 2, 0)
        hi = jnp.minimum(row + win // 2 - 1, seq_len - 1)
        cnt = (hi - lo + 1).astype(F32)
        p = s[HALO:HALO + r] / cnt - u[HALO:HALO + r]
        outs.append(jnp.dot(p.astype(BF16), pw_ref[g], preferred_element_type=F32))
    ypool_ref[...] = (jnp.concatenate(outs, axis=1) * ps_ref[...]).astype(BF16)

    for c in range(r // CONV_SUB):
        base = HALO - CONV_TAPS // 2 + c * CONV_SUB
        acc = jnp.zeros((CONV_SUB, MIXER_WIDTH), F32)
        for j in range(CONV_TAPS):
            acc = acc + conv_scr[base + j:base + j + CONV_SUB, :] * dw_ref[j:j + 1, :]
        acc_scr[c * CONV_SUB:(c + 1) * CONV_SUB, :] = acc
    y = acc_scr[...] + dwb_ref[...]
    mu = jnp.mean(y, axis=-1, keepdims=True)
    yc = y - mu
    yn = yc * lax.rsqrt(jnp.mean(yc * yc, axis=-1, keepdims=True) + EPS) * lng_ref[...] + lnb_ref[...]
    act = yn * jax.nn.sigmoid(yn)
    yconv_ref[...] = (jnp.dot(act.astype(BF16), cw_ref[...], preferred_element_type=F32)
                      + cwb_ref[...]).astype(BF16)


def _seqmix(upool, glu, pool_w, pool_scale, conv_dw, conv_dw_b, conv_ln_g, conv_ln_b,
            conv_pw, conv_pw_b, layer, seq_len, name):
    m_rows = upool.shape[0]
    w = MIXER_WIDTH
    r = SEQ_CHUNK
    n_chunks = m_rows // r
    hb = r // HALO
    n_halo_blocks = m_rows // HALO
    prev_spec = lambda: pl.BlockSpec((HALO, w), lambda i: (jnp.maximum(i * hb - 1, 0), 0))
    cur_spec = lambda: pl.BlockSpec((r, w), lambda i: (i, 0))
    next_spec = lambda: pl.BlockSpec((HALO, w), lambda i: (jnp.minimum((i + 1) * hb, n_halo_blocks - 1), 0))
    vec = lambda a: a.reshape(DEPTH, 1, w)
    vec_spec = lambda: pl.BlockSpec((None, 1, w), lambda i: (layer, 0, 0))
    return pl.pallas_call(
        functools.partial(_seqmix_kernel, seq_len // r),
        grid=(n_chunks,),
        in_specs=[prev_spec(), cur_spec(), next_spec(), prev_spec(), cur_spec(), next_spec(),
                  pl.BlockSpec((None, len(POOL_WINDOWS), POOL_GROUP_DIM, POOL_GROUP_DIM),
                               lambda i: (layer, 0, 0, 0)),
                  vec_spec(),
                  pl.BlockSpec((None, CONV_TAPS, w), lambda i: (layer, 0, 0)),
                  vec_spec(), vec_spec(), vec_spec(),
                  pl.BlockSpec((None, w, w), lambda i: (layer, 0, 0)),
                  vec_spec()],
        out_specs=[cur_spec(), cur_spec()],
        out_shape=[jax.ShapeDtypeStruct((m_rows, w), BF16)] * 2,
        scratch_shapes=[pltpu.VMEM((r + 2 * HALO, w), F32), pltpu.VMEM((r + 2 * HALO, w), F32),
                        pltpu.VMEM((r, w), F32)],
        compiler_params=_params("arbitrary"),
        name=name,
    )(upool, upool, upool, glu, glu, glu, pool_w, vec(pool_scale), conv_dw, vec(conv_dw_b),
      vec(conv_ln_g), vec(conv_ln_b), conv_pw, vec(conv_pw_b))


def _fourier_kernel(seq_len, scale, u_ref, pos_ref, ch_ref, fw_ref, o_ref, ab_scr, pos_scr):
    b = pl.program_id(1)

    @pl.when(b == 0)
    def _():
        pos_scr[...] = pos_ref[...].astype(BF16)

    @pl.when(pl.program_id(0) == 0)
    def _():
        ch = ch_ref[...].astype(BF16)
        for h in range(MIXER_WIDTH // FOURIER_HEAD_DIM):
            lanes = slice(h * FOURIER_HEAD_DIM, (h + 1) * FOURIER_HEAD_DIM)
            a = jnp.dot(u_ref[:, lanes], ch, preferred_element_type=F32)
            ab_scr[b, 0:seq_len, lanes] = a[:, :FOURIER_HEAD_DIM].astype(BF16)
            ab_scr[b, seq_len:2 * seq_len, lanes] = a[:, FOURIER_HEAD_DIM:].astype(BF16)

    f = jnp.dot(pos_scr[...], ab_scr[b], preferred_element_type=F32) * scale
    o_ref[...] = jnp.dot(f.astype(BF16), fw_ref[...], preferred_element_type=F32).astype(BF16)


def _fourier(ufour, fourier_w, layer, seq_len, name):
    m_rows = ufour.shape[0]
    w = MIXER_WIDTH
    batch = m_rows // seq_len
    tl = min(seq_len, 256)
    blocks = seq_len // tl
    pos_tab, ch_tab = _dft_tables(seq_len)
    scale = float(1.0 / np.sqrt(seq_len * FOURIER_HEAD_DIM))
    return pl.pallas_call(
        functools.partial(_fourier_kernel, seq_len, scale),
        grid=(blocks, batch),
        in_specs=[pl.BlockSpec((seq_len, w), lambda j, b: (jnp.where(j == 0, b, batch - 1), 0)),
                  pl.BlockSpec((tl, 2 * seq_len), lambda j, b: (j, 0)),
                  pl.BlockSpec((FOURIER_HEAD_DIM, 2 * FOURIER_HEAD_DIM), lambda j, b: (0, 0)),
                  pl.BlockSpec((None, w, w), lambda j, b: (layer, 0, 0))],
        out_specs=pl.BlockSpec((tl, w), lambda j, b: (b * blocks + j, 0)),
        out_shape=jax.ShapeDtypeStruct((m_rows, w), BF16),
        scratch_shapes=[pltpu.VMEM((batch, 2 * seq_len, w), BF16), pltpu.VMEM((tl, 2 * seq_len), BF16)],
        compiler_params=_params("arbitrary", "arbitrary"),
        name=name,
    )(ufour, pos_tab, ch_tab, fourier_w)


def _attn_kernel(lam_init, cached, *refs):
    if cached:
        q_ref, k_ref, v_ref, ck_ref, cv_ref, lam_ref, gs_ref, o_ref = refs
    else:
        q_ref, k_ref, v_ref, lam_ref, gs_ref, o_ref = refs
    tq = q_ref.shape[0]
    lv = lam_ref[...]
    lam = (jnp.exp(jnp.sum(lv[0:1] * lv[1:2], axis=-1, keepdims=True))
           - jnp.exp(jnp.sum(lv[2:3] * lv[3:4], axis=-1, keepdims=True)) + lam_init)

    q = q_ref[...]
    lane = lax.broadcasted_iota(jnp.int32, (1, V_DIM), 1)
    zero = jnp.zeros_like(q)
    qq = jnp.concatenate([jnp.where(lane < QK_DIM, q, zero),
                          jnp.where(lane >= QK_DIM, q, zero)], axis=0)
    nt = (((1,), (1,)), ((), ()))
    scores = [lax.dot_general(qq, k_ref[...], nt, preferred_element_type=F32)]
    values = [v_ref[...]]
    if cached:
        scores.append(lax.dot_general(qq, ck_ref[...].astype(BF16), nt, preferred_element_type=F32))
        values.append(cv_ref[...].astype(BF16))
    m = functools.reduce(jnp.maximum, [s.max(axis=-1, keepdims=True) for s in scores])
    es = [jnp.exp(s - m) for s in scores]
    denom = functools.reduce(jnp.add, [e.sum(axis=-1, keepdims=True) for e in es])
    inv = 1.0 / denom
    o = jnp.zeros((tq, V_DIM), F32)
    for e, v in zip(es, values):
        a = e[:tq] * inv[:tq] - lam * (e[tq:] * inv[tq:])
        o = o + jnp.dot(a.astype(BF16), v, preferred_element_type=F32)
    o_ref[...] = (_rms(o, gs_ref[...]) * (1.0 - lam_init)).astype(BF16)


def _attention(q, k, v, cache_k, cache_v, lam, g_subln, layer, lam_init, seq_len, tq, name):
    m_rows = q.shape[0]
    batch = m_rows // seq_len
    qb = seq_len // tq
    cached = cache_k is not None
    in_specs = [pl.BlockSpec((tq, V_DIM), lambda b, h, i: (b * qb + i, h)),
                pl.BlockSpec((seq_len, V_DIM), lambda b, h, i: (b, h)),
                pl.BlockSpec((seq_len, V_DIM), lambda b, h, i: (b, h))]
    args = [q, k, v]
    if cached:
        past = cache_k.shape[2]
        spec = lambda: pl.BlockSpec((None, None, past, V_DIM), lambda b, h, i: (b, layer, 0, h))
        in_specs += [spec(), spec()]
        args += [cache_k, cache_v]
    in_specs += [pl.BlockSpec((None, 4, QK_DIM), lambda b, h, i: (layer, 0, 0)),
                 pl.BlockSpec((None, 1, V_DIM), lambda b, h, i: (layer, 0, 0))]
    args += [lam, g_subln.reshape(DEPTH, 1, V_DIM)]
    return pl.pallas_call(
        functools.partial(_attn_kernel, lam_init, cached),
        grid=(batch, N_HEADS, qb),
        in_specs=in_specs,
        out_specs=pl.BlockSpec((tq, V_DIM), lambda b, h, i: (b * qb + i, h)),
        out_shape=jax.ShapeDtypeStruct((m_rows, N_HEADS * V_DIM), BF16),
        compiler_params=_params("arbitrary", "arbitrary", "arbitrary"),
        name=name,
    )(*args)


def _outproj_kernel(x_ref, p0, p1, p2, p3, w_ref, g1_ref, sh_ref, sc_ref, gn_ref, x1_ref, h2_ref):
    mix = jnp.concatenate([p0[...], p1[...], p2[...], p3[...]], axis=1)
    a = jnp.dot(mix, w_ref[...], preferred_element_type=F32)
    x1 = x_ref[...] + g1_ref[...] * a
    x1_ref[...] = x1
    h2_ref[...] = (_rms(x1, gn_ref[...]) * (1.0 + sc_ref[...]) + sh_ref[...]).astype(BF16)


def _outproj(x, parts, w_out, mod, g_norm2, layer, row_of, tm, name):
    m_rows = x.shape[0]
    w = MIXER_WIDTH
    row_spec = lambda: pl.BlockSpec((tm, D_MODEL), lambda i: (i, 0))
    part_spec = lambda: pl.BlockSpec((tm, w), lambda i: (i, 0))
    return pl.pallas_call(
        _outproj_kernel,
        grid=(m_rows // tm,),
        in_specs=[row_spec(), part_spec(), part_spec(), part_spec(), part_spec(),
                  pl.BlockSpec((None, D_MODEL, D_MODEL), lambda i: (layer, 0, 0)),
                  _mod_spec(layer, 2, row_of), _mod_spec(layer, 3, row_of), _mod_spec(layer, 4, row_of),
                  pl.BlockSpec((None, 1, D_MODEL), lambda i: (layer, 0, 0))],
        out_specs=[row_spec(), row_spec()],
        out_shape=[jax.ShapeDtypeStruct((m_rows, D_MODEL), F32),
                   jax.ShapeDtypeStruct((m_rows, D_MODEL), BF16)],
        compiler_params=_params("arbitrary"),
        name=name,
    )(x, *parts, w_out, mod, mod, mod, g_norm2.reshape(DEPTH, 1, D_MODEL))


def _ffn_kernel(h_ref, x_ref, wg_ref, wu_ref, wd_ref, g2_ref, o_ref, acc_scr):
    f = pl.program_id(1)

    @pl.when(f == 0)
    def _():
        acc_scr[...] = jnp.zeros_like(acc_scr)

    h = h_ref[...]
    g = jnp.dot(h, wg_ref[...], preferred_element_type=F32)
    u = jnp.dot(h, wu_ref[...], preferred_element_type=F32)
    a = (g * jax.nn.sigmoid(g) * u).astype(BF16)
    acc_scr[...] += jnp.dot(a, wd_ref[...], preferred_element_type=F32)

    @pl.when(f == pl.num_programs(1) - 1)
    def _():
        o_ref[...] = x_ref[...] + g2_ref[...] * acc_scr[...]


def _ffn(h2, x1, w_gate, w_up, w_down, mod, layer, row_of, tm, tf, name):
    m_rows = x1.shape[0]
    return pl.pallas_call(
        _ffn_kernel,
        grid=(m_rows // tm, D_FF // tf),
        in_specs=[pl.BlockSpec((tm, D_MODEL), lambda i, f: (i, 0)),
                  pl.BlockSpec((tm, D_MODEL), lambda i, f: (i, 0)),
                  pl.BlockSpec((None, D_MODEL, tf), lambda i, f: (layer, 0, f)),
                  pl.BlockSpec((None, D_MODEL, tf), lambda i, f: (layer, 0, f)),
                  pl.BlockSpec((None, tf, D_MODEL), lambda i, f: (layer, f, 0)),
                  _mod_spec(layer, 5, row_of)],
        out_specs=pl.BlockSpec((tm, D_MODEL), lambda i, f: (i, 0)),
        out_shape=jax.ShapeDtypeStruct((m_rows, D_MODEL), F32),
        scratch_shapes=[pltpu.VMEM((tm, D_MODEL), F32)],
        compiler_params=_params("arbitrary", "arbitrary"),
        name=name,
    )(h2, x1, w_gate, w_up, w_down, mod)


def kernel(x_prompt, x_sample, cache_k, cache_v, c, c_ctx, w_ada, b_ada, g_norm1, w_in, pool_w, pool_scale, g_q, g_k, lam, g_subln, conv_dw, conv_dw_b, conv_ln_g, conv_ln_b, conv_pw, conv_pw_b, fourier_w, w_out, g_norm2, w_gate, w_up, w_down):
    batch, seq, _ = x_prompt.shape
    dec_batch, dec_seq, _ = x_sample.shape
    past = cache_k.shape[2]
    tm = 512

    cc = jnp.concatenate([c_ctx[None, :], c, jnp.zeros((8 - 1 - dec_batch, D_MODEL), F32)], axis=0)
    mod = _modulation(cc, w_ada, b_ada).reshape(DEPTH, 8, 1, 6 * D_MODEL)

    bf = lambda a: a.astype(BF16)
    w_in_b, w_out_b, w_gate_b, w_up_b, w_down_b = bf(w_in), bf(w_out), bf(w_gate), bf(w_up), bf(w_down)
    pool_w_b, conv_pw_b16, fourier_w_b = bf(pool_w), bf(conv_pw), bf(fourier_w)
    ck = cache_k.reshape(dec_batch, DEPTH, past, N_HEADS * V_DIM)
    cv = cache_v.reshape(dec_batch, DEPTH, past, N_HEADS * V_DIM)

    streams = {
        "ctx": dict(x=x_prompt.reshape(batch * seq, D_MODEL), seq=seq, latent=False,
                    row_of=lambda i: 0),
        "lat": dict(x=x_sample.reshape(dec_batch * dec_seq, D_MODEL), seq=dec_seq, latent=True,
                    row_of=lambda i: 1 + i // (dec_seq // tm)),
    }
    new_k, new_v = [], []
    for layer in range(DEPTH):
        lam_init = 0.8 - 0.6 * float(np.exp(-0.3 * layer))
        for name, st in streams.items():
            latent, seq_len = st["latent"], st["seq"]
            outs = _inproj(st["x"], mod, g_norm1, w_in_b, g_q, g_k, layer, seq_len, latent, tm)
            upool, q, k, v, glu, ufour = outs[:6]
            if not latent:
                new_k.append(outs[6].reshape(batch, seq, N_HEADS, 2, QK_DIM))
                new_v.append(outs[7].reshape(batch, seq, N_HEADS, V_DIM))
            y_pool, y_conv = _seqmix(upool, glu, pool_w_b, pool_scale, conv_dw, conv_dw_b, conv_ln_g,
                                     conv_ln_b, conv_pw_b16, conv_pw_b, layer, seq_len, "seqmix_" + name)
            y_four = _fourier(ufour, fourier_w_b, layer, seq_len, "fourier_" + name)
            att = _attention(q, k, v, ck if latent else None, cv if latent else None, lam, g_subln,
                             layer, lam_init, seq_len, 256, "attn_" + name)
            x1, h2 = _outproj(st["x"], (y_pool, att, y_conv, y_four), w_out_b, mod, g_norm2, layer,
                              st["row_of"], tm, "outproj_" + name)
            st["x"] = _ffn(h2, x1, w_gate_b, w_up_b, w_down_b, mod, layer, st["row_of"], tm, 512,
                           "ffn_" + name)
    y_prompt = streams["ctx"]["x"].reshape(batch, seq, D_MODEL)
    y_sample = streams["lat"]["x"].reshape(dec_batch, dec_seq, D_MODEL)
    return (y_prompt, y_sample, jnp.stack(new_k, axis=1), jnp.stack(new_v, axis=1))
```

```python
import functools

import numpy as np
import jax
import jax.numpy as jnp
from jax import lax
from jax.experimental import pallas as pl
from jax.experimental.pallas import tpu as pltpu

D_MODEL = 2048
DEPTH = 2
GRID_W = 64
MIXER_WIDTH = 512
POOL_GROUP_DIM = 128
POOL_WINDOWS = (2, 4, 8, 16)
N_HEADS = 4
V_DIM = 128
QK_DIM = 64
ROPE_AXIS_DIM = 32
ROPE_BASE = 10000.0
CONV_TAPS = 31
FOURIER_HEAD_DIM = 128
IN_COLS = 7 * MIXER_WIDTH
D_FF = 5632
EPS = 1e-6

LANES = 128
SUBLANES = 8
HALO = 16
SEQ_CHUNK = 256
CONV_SUB = 32
KEY_CHUNK = 512
VT_ROWS = V_DIM + 2 * SUBLANES
LOG2E = 1.4426950408889634
VMEM_LIMIT = 56 * 1024 * 1024

BF16 = jnp.bfloat16
F32 = jnp.float32


def _params(*semantics):
    return pltpu.CompilerParams(dimension_semantics=semantics, vmem_limit_bytes=VMEM_LIMIT)


@functools.lru_cache(maxsize=None)
def _rope_tables(seq_len):
    t = np.arange(seq_len)
    pos = np.stack([(t // GRID_W), (t % GRID_W)], axis=0).astype(np.float64)
    half = ROPE_AXIS_DIM // 2
    inv = 1.0 / (ROPE_BASE ** (np.arange(0, ROPE_AXIS_DIM, 2, dtype=np.float64) / ROPE_AXIS_DIM))
    lane = np.arange(LANES)
    d = lane % QK_DIM
    axis = d // ROPE_AXIS_DIM
    freq = d % half
    ang = pos[axis, :].T * inv[freq][None, :]
    first = ((d % ROPE_AXIS_DIM) < half)[None, :]
    cos = np.cos(ang)
    sin = np.sin(ang)
    sin_next = np.where(first, -sin, 0.0)
    sin_prev = np.where(first, 0.0, sin)
    return (cos.astype(np.float32), sin_next.astype(np.float32), sin_prev.astype(np.float32))


@functools.lru_cache(maxsize=None)
def _dft_tables(seq_len):
    def cs(n):
        k = np.arange(n)
        m = (k[:, None] * k[None, :]) % n
        a = 2.0 * np.pi * m.astype(np.float64) / n
        return np.cos(a), np.sin(a)
    cl, sl = cs(seq_len)
    cc, sc = cs(FOURIER_HEAD_DIM)
    pos = np.concatenate([cl, -sl], axis=1).astype(np.float32)
    ch = np.concatenate([cc, sc], axis=1).astype(np.float32)
    return pos, ch


@functools.lru_cache(maxsize=None)
def _group_mean_matrix():
    g = np.arange(MIXER_WIDTH) // QK_DIM
    p = (g[:, None] == g[None, :]).astype(np.float32) / QK_DIM
    return jnp.asarray(p, dtype=BF16)


def _mod_kernel(c_ref, w_ref, b_ref, o_ref):
    c = c_ref[...]
    s = (c * jax.nn.sigmoid(c)).astype(BF16)
    o_ref[...] = jnp.dot(s, w_ref[...].astype(BF16), preferred_element_type=F32) + b_ref[...]


def _modulation(cc, w_ada, b_ada):
    tn = 1024
    n_cols = 6 * D_MODEL
    return pl.pallas_call(
        _mod_kernel,
        grid=(DEPTH, n_cols // tn),
        in_specs=[
            pl.BlockSpec((8, D_MODEL), lambda l, j: (0, 0)),
            pl.BlockSpec((None, D_MODEL, tn), lambda l, j: (l, 0, j)),
            pl.BlockSpec((None, 1, tn), lambda l, j: (l, 0, j)),
        ],
        out_specs=pl.BlockSpec((None, 8, tn), lambda l, j: (l, 0, j)),
        out_shape=jax.ShapeDtypeStruct((DEPTH, 8, n_cols), F32),
        compiler_params=_params("arbitrary", "arbitrary"),
        name="adaln_mod",
    )(cc, w_ada, b_ada.reshape(DEPTH, 1, n_cols))


def _mod_spec(layer, chunk, row_of_block):
    return pl.BlockSpec((None, None, 1, D_MODEL),
                        lambda i, *_: (layer, row_of_block(i), 0, chunk))


def _rms(x, gain):
    return x * lax.rsqrt(jnp.mean(x * x, axis=-1, keepdims=True) + EPS) * gain


def _rope(t, cos, sin_next, sin_prev):
    outs = []
    for c in range(MIXER_WIDTH // LANES):
        tc = t[:, c * LANES:(c + 1) * LANES]
        outs.append(tc * cos
                    + pltpu.roll(tc, LANES - ROPE_AXIS_DIM // 2, axis=1) * sin_next
                    + pltpu.roll(tc, ROPE_AXIS_DIM // 2, axis=1) * sin_prev)
    return jnp.concatenate(outs, axis=1)


def _inproj_kernel(latent, *refs):
    if latent:
        (x_ref, sh_ref, sc_ref, g1_ref, w_ref, gq_ref, gk_ref, p_ref, cos_ref, sn_ref, sp_ref,
         upool_ref, q_ref, k_ref, v_ref, glu_ref, ufour_ref, h_scr) = refs
    else:
        (x_ref, sh_ref, sc_ref, g1_ref, w_ref, gq_ref, gk_ref, p_ref,
         upool_ref, q_ref, k_ref, v_ref, glu_ref, ufour_ref, kf_ref, vf_ref, h_scr) = refs
    w = MIXER_WIDTH
    h = _rms(x_ref[...], g1_ref[...]) * (1.0 + sc_ref[...]) + sh_ref[...]
    h_scr[...] = h.astype(BF16)

    def proj(n):
        return jnp.dot(h_scr[...], w_ref[:, n * w:(n + 1) * w], preferred_element_type=F32)

    def qk_norm(z, gain_ref):
        ms = jnp.dot((z * z).astype(BF16), p_ref[...], preferred_element_type=F32)
        return z * lax.rsqrt(ms + EPS) * gain_ref[...]

    upool_ref[...] = proj(0)

    qn = qk_norm(proj(1), gq_ref)
    if latent:
        qn = _rope(qn, cos_ref[...], sn_ref[...], sp_ref[...])
    q_ref[...] = (qn * (QK_DIM ** -0.5 * LOG2E)).astype(BF16)

    kn = qk_norm(proj(2), gk_ref)
    if latent:
        kn = _rope(kn, cos_ref[...], sn_ref[...], sp_ref[...])
    else:
        kf_ref[...] = kn
    k_ref[...] = kn.astype(BF16)

    v = proj(3)
    if not latent:
        vf_ref[...] = v
    v_ref[...] = v.astype(BF16)

    glu_ref[...] = proj(4) * jax.nn.sigmoid(proj(5))
    ufour_ref[...] = proj(6).astype(BF16)


def _inproj(x, mod, g_norm1, w_in, g_q, g_k, layer, seq_len, latent, tm):
    m_rows = x.shape[0]
    w = MIXER_WIDTH
    if latent:
        row_of = lambda i: 1 + i // (seq_len // tm)
    else:
        row_of = lambda i: 0
    row_spec = lambda: pl.BlockSpec((tm, w), lambda i: (i, 0))
    vec_spec = lambda: pl.BlockSpec((1, w), lambda i: (0, 0))
    in_specs = [
        pl.BlockSpec((tm, D_MODEL), lambda i: (i, 0)),
        _mod_spec(layer, 0, row_of),
        _mod_spec(layer, 1, row_of),
        pl.BlockSpec((None, 1, D_MODEL), lambda i: (layer, 0, 0)),
        pl.BlockSpec((None, D_MODEL, IN_COLS), lambda i: (layer, 0, 0), pipeline_mode=pl.Buffered(1)),
        vec_spec(), vec_spec(),
        pl.BlockSpec((w, w), lambda i: (0, 0)),
    ]
    args = [x, mod, mod, g_norm1.reshape(DEPTH, 1, D_MODEL), w_in,
            jnp.tile(g_q[layer], w // QK_DIM).reshape(1, w),
            jnp.tile(g_k[layer], w // QK_DIM).reshape(1, w),
            _group_mean_matrix()]
    out_shape = [jax.ShapeDtypeStruct((m_rows, w), F32)] + \
                [jax.ShapeDtypeStruct((m_rows, w), BF16)] * 3 + \
                [jax.ShapeDtypeStruct((m_rows, w), F32), jax.ShapeDtypeStruct((m_rows, w), BF16)]
    out_specs = [row_spec() for _ in range(6)]
    if latent:
        blocks_per_seq = seq_len // tm
        for tab in _rope_tables(seq_len):
            in_specs.append(pl.BlockSpec((tm, LANES), lambda i: (i % blocks_per_seq, 0)))
            args.append(jnp.asarray(tab))
    else:
        out_shape += [jax.ShapeDtypeStruct((m_rows, w), F32)] * 2
        out_specs += [row_spec(), row_spec()]
    return pl.pallas_call(
        functools.partial(_inproj_kernel, latent),
        grid=(m_rows // tm,),
        in_specs=in_specs,
        out_specs=out_specs,
        out_shape=out_shape,
        scratch_shapes=[pltpu.VMEM((tm, D_MODEL), BF16)],
        compiler_params=_params("arbitrary"),
        name="inproj_lat" if latent else "inproj_ctx",
    )(*args)


def _shift_rows(x, s):
    return pltpu.roll(x, s % x.shape[0], axis=0)


def _seqmix_kernel(chunks_per_seq, up_prev, up_cur, up_next, gl_prev, gl_cur, gl_next,
                   pw_ref, ps_ref, dw_ref, dwb_ref, lng_ref, lnb_ref, cw_ref, cwb_ref,
                   ypool_ref, yconv_ref, pool_scr, conv_scr, shift_scr, acc_scr):
    i = pl.program_id(0)
    r = SEQ_CHUNK
    first = (i % chunks_per_seq) == 0
    last = (i % chunks_per_seq) == chunks_per_seq - 1
    zero_halo = jnp.zeros((HALO, MIXER_WIDTH), F32)

    def fill(scr, prev, cur, nxt):
        scr[0:HALO, :] = jnp.where(first, zero_halo, prev[...])
        scr[HALO:HALO + r, :] = cur[...]
        scr[HALO + r:2 * HALO + r, :] = jnp.where(last, zero_halo, nxt[...])

    fill(pool_scr, up_prev, up_cur, up_next)
    fill(conv_scr, gl_prev, gl_cur, gl_next)

    row = lax.broadcasted_iota(jnp.int32, (r, 1), 0) + (i % chunks_per_seq) * r
    seq_len = chunks_per_seq * r
    outs = []
    for g, win in enumerate(POOL_WINDOWS):
        lanes = slice(g * POOL_GROUP_DIM, (g + 1) * POOL_GROUP_DIM)
        u = pool_scr[:, lanes]
        s = _shift_rows(u, 1) + u
        span = 2
        while span < win:
            s = _shift_rows(s, span // 2) + _shift_rows(s, -(span // 2))
            span *= 2
        lo = jnp.maximum(row - win // 2, 0)
        hi = jnp.minimum(row + win // 2 - 1, seq_len - 1)
        cnt = (hi - lo + 1).astype(F32)
        p = s[HALO:HALO + r] / cnt - u[HALO:HALO + r]
        outs.append(jnp.dot(p.astype(BF16), pw_ref[g], preferred_element_type=F32))
    ypool_ref[...] = (jnp.concatenate(outs, axis=1) * ps_ref[...]).astype(BF16)

    shifted_rows = r + SUBLANES * (CONV_TAPS // SUBLANES)
    for lo in range(1, SUBLANES):
        shift_scr[lo - 1] = conv_scr[lo:lo + shifted_rows, :]
    groups = CONV_SUB // SUBLANES
    for c in range(r // CONV_SUB):
        acc = jnp.zeros((groups, SUBLANES, MIXER_WIDTH), F32)
        for j in range(CONV_TAPS):
            hi, lo = divmod(j + 1, SUBLANES)
            src = conv_scr if lo == 0 else shift_scr.at[lo - 1]
            start = c * CONV_SUB + SUBLANES * hi
            rows = src[start:start + CONV_SUB, :].reshape(groups, SUBLANES, MIXER_WIDTH)
            acc = acc + rows * dw_ref[j]
        acc_scr[c * CONV_SUB:(c + 1) * CONV_SUB, :] = acc.reshape(CONV_SUB, MIXER_WIDTH)
    y = acc_scr[...] + dwb_ref[...]
    mu = jnp.mean(y, axis=-1, keepdims=True)
    yc = y - mu
    yn = yc * lax.rsqrt(jnp.mean(yc * yc, axis=-1, keepdims=True) + EPS) * lng_ref[...] + lnb_ref[...]
    act = yn * jax.nn.sigmoid(yn)
    yconv_ref[...] = (jnp.dot(act.astype(BF16), cw_ref[...], preferred_element_type=F32)
                      + cwb_ref[...]).astype(BF16)


def _seqmix(upool, glu, pool_w, pool_scale, conv_dw, conv_dw_b, conv_ln_g, conv_ln_b,
            conv_pw, conv_pw_b, layer, seq_len, name):
    m_rows = upool.shape[0]
    w = MIXER_WIDTH
    r = SEQ_CHUNK
    n_chunks = m_rows // r
    hb = r // HALO
    n_halo_blocks = m_rows // HALO
    prev_spec = lambda: pl.BlockSpec((HALO, w), lambda i: (jnp.maximum(i * hb - 1, 0), 0))
    cur_spec = lambda: pl.BlockSpec((r, w), lambda i: (i, 0))
    next_spec = lambda: pl.BlockSpec((HALO, w), lambda i: (jnp.minimum((i + 1) * hb, n_halo_blocks - 1), 0))
    vec = lambda a: a.reshape(DEPTH, 1, w)
    vec_spec = lambda: pl.BlockSpec((None, 1, w), lambda i: (layer, 0, 0))
    return pl.pallas_call(
        functools.partial(_seqmix_kernel, seq_len // r),
        grid=(n_chunks,),
        in_specs=[prev_spec(), cur_spec(), next_spec(), prev_spec(), cur_spec(), next_spec(),
                  pl.BlockSpec((None, len(POOL_WINDOWS), POOL_GROUP_DIM, POOL_GROUP_DIM),
                               lambda i: (layer, 0, 0, 0)),
                  vec_spec(),
                  pl.BlockSpec((None, CONV_TAPS, SUBLANES, w), lambda i: (layer, 0, 0, 0)),
                  vec_spec(), vec_spec(), vec_spec(),
                  pl.BlockSpec((None, w, w), lambda i: (layer, 0, 0)),
                  vec_spec()],
        out_specs=[cur_spec(), cur_spec()],
        out_shape=[jax.ShapeDtypeStruct((m_rows, w), BF16)] * 2,
        scratch_shapes=[pltpu.VMEM((r + 2 * HALO, w), F32), pltpu.VMEM((r + 2 * HALO, w), F32),
                        pltpu.VMEM((SUBLANES - 1, r + SUBLANES * (CONV_TAPS // SUBLANES), w), F32),
                        pltpu.VMEM((r, w), F32)],
        compiler_params=_params("arbitrary"),
        name=name,
    )(upool, upool, upool, glu, glu, glu, pool_w, vec(pool_scale),
      jnp.broadcast_to(conv_dw[:, :, None, :], (DEPTH, CONV_TAPS, SUBLANES, w)), vec(conv_dw_b),
      vec(conv_ln_g), vec(conv_ln_b), conv_pw, vec(conv_pw_b))


def _fourier_kernel(seq_len, scale, u_ref, pos_ref, ch_ref, fw_ref, o_ref, ab_scr, pos_scr):
    b = pl.program_id(1)

    @pl.when(b == 0)
    def _():
        pos_scr[...] = pos_ref[...].astype(BF16)

    @pl.when(pl.program_id(0) == 0)
    def _():
        ch = ch_ref[...].astype(BF16)
        for h in range(MIXER_WIDTH // FOURIER_HEAD_DIM):
            lanes = slice(h * FOURIER_HEAD_DIM, (h + 1) * FOURIER_HEAD_DIM)
            a = jnp.dot(u_ref[:, lanes], ch, preferred_element_type=F32)
            ab_scr[b, 0:seq_len, lanes] = a[:, :FOURIER_HEAD_DIM].astype(BF16)
            ab_scr[b, seq_len:2 * seq_len, lanes] = a[:, FOURIER_HEAD_DIM:].astype(BF16)

    f = jnp.dot(pos_scr[...], ab_scr[b], preferred_element_type=F32) * scale
    o_ref[...] = jnp.dot(f.astype(BF16), fw_ref[...], preferred_element_type=F32).astype(BF16)


def _fourier(ufour, fourier_w, layer, seq_len, name):
    m_rows = ufour.shape[0]
    w = MIXER_WIDTH
    batch = m_rows // seq_len
    tl = min(seq_len, 256)
    blocks = seq_len // tl
    pos_tab, ch_tab = _dft_tables(seq_len)
    scale = float(1.0 / np.sqrt(seq_len * FOURIER_HEAD_DIM))
    return pl.pallas_call(
        functools.partial(_fourier_kernel, seq_len, scale),
        grid=(blocks, batch),
        in_specs=[pl.BlockSpec((seq_len, w), lambda j, b: (jnp.where(j == 0, b, batch - 1), 0)),
                  pl.BlockSpec((tl, 2 * seq_len), lambda j, b: (j, 0)),
                  pl.BlockSpec((FOURIER_HEAD_DIM, 2 * FOURIER_HEAD_DIM), lambda j, b: (0, 0)),
                  pl.BlockSpec((None, w, w), lambda j, b: (layer, 0, 0))],
        out_specs=pl.BlockSpec((tl, w), lambda j, b: (b * blocks + j, 0)),
        out_shape=jax.ShapeDtypeStruct((m_rows, w), BF16),
        scratch_shapes=[pltpu.VMEM((batch, 2 * seq_len, w), BF16), pltpu.VMEM((tl, 2 * seq_len), BF16)],
        compiler_params=_params("arbitrary", "arbitrary"),
        name=name,
    )(ufour, pos_tab, ch_tab, fourier_w)


def _attn_kernel(lam_init, cached, heads, *refs):
    if cached:
        q_ref, k_ref, v_ref, ck_ref, cv_ref, lam_ref, gs_ref, o_ref, vt_scr, s_scr, ckb_scr = refs
    else:
        q_ref, k_ref, v_ref, lam_ref, gs_ref, o_ref, vt_scr, s_scr = refs
    tq = q_ref.shape[0]
    own = k_ref.shape[0]
    past = ck_ref.shape[0] if cached else 0

    @pl.when(pl.program_id(2) == 0)
    def _():
        for h in range(heads):
            cols = slice(h * V_DIM, (h + 1) * V_DIM)
            base = h * VT_ROWS
            vt_scr[base:base + V_DIM, 0:own] = v_ref[:, cols].astype(F32).T.astype(BF16)
            vt_scr[base + V_DIM:base + VT_ROWS, :] = jnp.ones((VT_ROWS - V_DIM, own + past), BF16)
            if cached:
                vt_scr[base:base + V_DIM, own:] = cv_ref[:, cols].T.astype(BF16)
                ckb_scr[:, cols] = ck_ref[:, cols].astype(BF16)

    lv = lam_ref[...]
    lam = (jnp.exp(jnp.sum(lv[0:1] * lv[1:2], axis=-1, keepdims=True))
           - jnp.exp(jnp.sum(lv[2:3] * lv[3:4], axis=-1, keepdims=True)) + lam_init)
    lane = lax.broadcasted_iota(jnp.int32, (1, V_DIM), 1)
    nt = (((1,), (1,)), ((), ()))
    segments = [(k_ref, 0, own)] + ([(ckb_scr, own, past)] if cached else [])
    items = [(h, key_ref, offset, c0, min(KEY_CHUNK, count - c0))
             for h in range(heads) for key_ref, offset, count in segments
             for c0 in range(0, count, KEY_CHUNK)]
    qqs = []
    for h in range(heads):
        q = q_ref[:, h * V_DIM:(h + 1) * V_DIM]
        zero = jnp.zeros_like(q)
        qqs.append(jnp.concatenate([jnp.where(lane < QK_DIM, q, zero),
                                    jnp.where(lane >= QK_DIM, q, zero)], axis=0))

    def scores(n):
        h, key_ref, _, c0, kc = items[n]
        s_scr[n % 2, 0:kc, :] = lax.dot_general(key_ref[c0:c0 + kc, h * V_DIM:(h + 1) * V_DIM], qqs[h], nt,
                                                preferred_element_type=F32)

    scores(0)
    parts = [[] for _ in range(heads)]
    for n, (h, _, offset, c0, kc) in enumerate(items):
        if n + 1 < len(items):
            scores(n + 1)
        s = s_scr[n % 2, 0:kc, :]
        mc = s.max(axis=0, keepdims=True)
        e = jnp.exp2(s - mc).astype(BF16)
        oc = jnp.dot(vt_scr[h * VT_ROWS:(h + 1) * VT_ROWS, offset + c0:offset + c0 + kc], e,
                     preferred_element_type=F32)
        parts[h].append((mc, oc))
    for h in range(heads):
        m = functools.reduce(jnp.maximum, [mc for mc, _ in parts[h]])
        acc = functools.reduce(jnp.add, [oc * jnp.exp2(mc - m) for mc, oc in parts[h]])
        inv = 1.0 / acc[V_DIM:V_DIM + 1]
        o_t = acc[:V_DIM, :tq] * inv[:, :tq] - lam * (acc[:V_DIM, tq:] * inv[:, tq:])
        o_ref[:, h * V_DIM:(h + 1) * V_DIM] = (_rms(o_t.T, gs_ref[...]) * (1.0 - lam_init)).astype(BF16)


def _attention(q, k, v, cache_k, cache_v, lam, g_subln, layer, lam_init, seq_len, tq, heads, name):
    m_rows = q.shape[0]
    batch = m_rows // seq_len
    qb = seq_len // tq
    cached = cache_k is not None
    hw = heads * V_DIM
    in_specs = [pl.BlockSpec((tq, hw), lambda b, h, i: (b * qb + i, h)),
                pl.BlockSpec((seq_len, hw), lambda b, h, i: (b, h)),
                pl.BlockSpec((seq_len, hw), lambda b, h, i: (b, h))]
    args = [q, k, v]
    past = 0
    scratch = []
    if cached:
        past = cache_k.shape[2]
        spec = lambda: pl.BlockSpec((None, None, past, hw), lambda b, h, i: (b, layer, 0, h))
        in_specs += [spec(), spec()]
        args += [cache_k, cache_v]
        scratch = [pltpu.VMEM((past, hw), BF16)]
    in_specs += [pl.BlockSpec((None, 4, QK_DIM), lambda b, h, i: (layer, 0, 0)),
                 pl.BlockSpec((None, 1, V_DIM), lambda b, h, i: (layer, 0, 0))]
    args += [lam, g_subln.reshape(DEPTH, 1, V_DIM)]
    return pl.pallas_call(
        functools.partial(_attn_kernel, lam_init, cached, heads),
        grid=(batch, N_HEADS // heads, qb),
        in_specs=in_specs,
        out_specs=pl.BlockSpec((tq, hw), lambda b, h, i: (b * qb + i, h)),
        out_shape=jax.ShapeDtypeStruct((m_rows, N_HEADS * V_DIM), BF16),
        scratch_shapes=[pltpu.VMEM((heads * VT_ROWS, seq_len + past), BF16),
                        pltpu.VMEM((2, min(KEY_CHUNK, seq_len), 2 * tq), F32)] + scratch,
        compiler_params=_params("arbitrary", "arbitrary", "arbitrary"),
        name=name,
    )(*args)


def _outproj_kernel(x_ref, p0, p1, p2, p3, w_ref, g1_ref, sh_ref, sc_ref, gn_ref, x1_ref, h2_ref):
    mix = jnp.concatenate([p0[...], p1[...], p2[...], p3[...]], axis=1)
    a = jnp.dot(mix, w_ref[...], preferred_element_type=F32)
    x1 = x_ref[...] + g1_ref[...] * a
    x1_ref[...] = x1
    h2_ref[...] = (_rms(x1, gn_ref[...]) * (1.0 + sc_ref[...]) + sh_ref[...]).astype(BF16)


def _outproj(x, parts, w_out, mod, g_norm2, layer, row_of, tm, name):
    m_rows = x.shape[0]
    w = MIXER_WIDTH
    row_spec = lambda: pl.BlockSpec((tm, D_MODEL), lambda i: (i, 0))
    part_spec = lambda: pl.BlockSpec((tm, w), lambda i: (i, 0))
    return pl.pallas_call(
        _outproj_kernel,
        grid=(m_rows // tm,),
        in_specs=[row_spec(), part_spec(), part_spec(), part_spec(), part_spec(),
                  pl.BlockSpec((None, D_MODEL, D_MODEL), lambda i: (layer, 0, 0)),
                  _mod_spec(layer, 2, row_of), _mod_spec(layer, 3, row_of), _mod_spec(layer, 4, row_of),
                  pl.BlockSpec((None, 1, D_MODEL), lambda i: (layer, 0, 0))],
        out_specs=[row_spec(), row_spec()],
        out_shape=[jax.ShapeDtypeStruct((m_rows, D_MODEL), F32),
                   jax.ShapeDtypeStruct((m_rows, D_MODEL), BF16)],
        compiler_params=_params("arbitrary"),
        name=name,
    )(x, *parts, w_out, mod, mod, mod, g_norm2.reshape(DEPTH, 1, D_MODEL))


def _ffn_kernel(h_ref, x_ref, wg_ref, wu_ref, wd_ref, g2_ref, o_ref, acc_scr):
    f = pl.program_id(1)

    @pl.when(f == 0)
    def _():
        acc_scr[...] = jnp.zeros_like(acc_scr)

    h = h_ref[...]
    g = jnp.dot(h, wg_ref[...], preferred_element_type=F32)
    u = jnp.dot(h, wu_ref[...], preferred_element_type=F32)
    a = (g * jax.nn.sigmoid(g) * u).astype(BF16)
    acc_scr[...] += jnp.dot(a, wd_ref[...], preferred_element_type=F32)

    @pl.when(f == pl.num_programs(1) - 1)
    def _():
        o_ref[...] = x_ref[...] + g2_ref[...] * acc_scr[...]


def _ffn(h2, x1, w_gate, w_up, w_down, mod, layer, row_of, tm, tf, name):
    m_rows = x1.shape[0]
    return pl.pallas_call(
        _ffn_kernel,
        grid=(m_rows // tm, D_FF // tf),
        in_specs=[pl.BlockSpec((tm, D_MODEL), lambda i, f: (i, 0)),
                  pl.BlockSpec((tm, D_MODEL), lambda i, f: (i, 0)),
                  pl.BlockSpec((None, D_MODEL, tf), lambda i, f: (layer, 0, f)),
                  pl.BlockSpec((None, D_MODEL, tf), lambda i, f: (layer, 0, f)),
                  pl.BlockSpec((None, tf, D_MODEL), lambda i, f: (layer, f, 0)),
                  _mod_spec(layer, 5, row_of)],
        out_specs=pl.BlockSpec((tm, D_MODEL), lambda i, f: (i, 0)),
        out_shape=jax.ShapeDtypeStruct((m_rows, D_MODEL), F32),
        scratch_shapes=[pltpu.VMEM((tm, D_MODEL), F32)],
        compiler_params=_params("arbitrary", "arbitrary"),
        name=name,
    )(h2, x1, w_gate, w_up, w_down, mod)


def kernel(x_prompt, x_sample, cache_k, cache_v, c, c_ctx, w_ada, b_ada, g_norm1, w_in, pool_w, pool_scale, g_q, g_k, lam, g_subln, conv_dw, conv_dw_b, conv_ln_g, conv_ln_b, conv_pw, conv_pw_b, fourier_w, w_out, g_norm2, w_gate, w_up, w_down):
    batch, seq, _ = x_prompt.shape
    dec_batch, dec_seq, _ = x_sample.shape
    past = cache_k.shape[2]
    tm = 512

    cc = jnp.concatenate([c_ctx[None, :], c, jnp.zeros((8 - 1 - dec_batch, D_MODEL), F32)], axis=0)
    mod = _modulation(cc, w_ada, b_ada).reshape(DEPTH, 8, 1, 6 * D_MODEL)

    bf = lambda a: a.astype(BF16)
    w_in_b, w_out_b, w_gate_b, w_up_b, w_down_b = bf(w_in), bf(w_out), bf(w_gate), bf(w_up), bf(w_down)
    pool_w_b, conv_pw_b16, fourier_w_b = bf(pool_w), bf(conv_pw), bf(fourier_w)
    ck = cache_k.reshape(dec_batch, DEPTH, past, N_HEADS * V_DIM)
    cv = cache_v.reshape(dec_batch, DEPTH, past, N_HEADS * V_DIM)

    streams = {
        "ctx": dict(x=x_prompt.reshape(batch * seq, D_MODEL), seq=seq, latent=False,
                    row_of=lambda i: 0),
        "lat": dict(x=x_sample.reshape(dec_batch * dec_seq, D_MODEL), seq=dec_seq, latent=True,
                    row_of=lambda i: 1 + i // (dec_seq // tm)),
    }
    new_k, new_v = [], []
    for layer in range(DEPTH):
        lam_init = 0.8 - 0.6 * float(np.exp(-0.3 * layer))
        for name, st in streams.items():
            latent, seq_len = st["latent"], st["seq"]
            outs = _inproj(st["x"], mod, g_norm1, w_in_b, g_q, g_k, layer, seq_len, latent, tm)
            upool, q, k, v, glu, ufour = outs[:6]
            if not latent:
                new_k.append(outs[6].reshape(batch, seq, N_HEADS, 2, QK_DIM))
                new_v.append(outs[7].reshape(batch, seq, N_HEADS, V_DIM))
            y_pool, y_conv = _seqmix(upool, glu, pool_w_b, pool_scale, conv_dw, conv_dw_b, conv_ln_g,
                                     conv_ln_b, conv_pw_b16, conv_pw_b, layer, seq_len, "seqmix_" + name)
            y_four = _fourier(ufour, fourier_w_b, layer, seq_len, "fourier_" + name)
            att = _attention(q, k, v, ck if latent else None, cv if latent else None, lam, g_subln,
                             layer, lam_init, seq_len, 512 if latent else seq_len, 1 if latent else N_HEADS,
                             "attn_" + name)
            x1, h2 = _outproj(st["x"], (y_pool, att, y_conv, y_four), w_out_b, mod, g_norm2, layer,
                              st["row_of"], tm, "outproj_" + name)
            st["x"] = _ffn(h2, x1, w_gate_b, w_up_b, w_down_b, mod, layer, st["row_of"], tm, 512,
                           "ffn_" + name)
    y_prompt = streams["ctx"]["x"].reshape(batch, seq, D_MODEL)
    y_sample = streams["lat"]["x"].reshape(dec_batch, dec_seq, D_MODEL)
    return (y_prompt, y_sample, jnp.stack(new_k, axis=1), jnp.stack(new_v, axis=1))
```

```python
import functools

import numpy as np
import jax
import jax.numpy as jnp
from jax import lax
from jax.experimental import pallas as pl
from jax.experimental.pallas import tpu as pltpu

D_MODEL = 2048
DEPTH = 2
GRID_W = 64
MIXER_WIDTH = 512
POOL_GROUP_DIM = 128
POOL_WINDOWS = (2, 4, 8, 16)
N_HEADS = 4
V_DIM = 128
QK_DIM = 64
ROPE_AXIS_DIM = 32
ROPE_BASE = 10000.0
CONV_TAPS = 31
FOURIER_HEAD_DIM = 128
IN_COLS = 7 * MIXER_WIDTH
D_FF = 5632
EPS = 1e-6

LANES = 128
SUBLANES = 8
HALO = 16
SEQ_CHUNK = 256
CONV_SUB = 32
KEY_CHUNK = 512
FFN_TF = 512
VT_ROWS = V_DIM + 2 * SUBLANES
LOG2E = 1.4426950408889634
VMEM_LIMIT = 56 * 1024 * 1024

BF16 = jnp.bfloat16
F32 = jnp.float32


def _params(*semantics):
    return pltpu.CompilerParams(dimension_semantics=semantics, vmem_limit_bytes=VMEM_LIMIT)


@functools.lru_cache(maxsize=None)
def _rope_tables(seq_len):
    t = np.arange(seq_len)
    pos = np.stack([(t // GRID_W), (t % GRID_W)], axis=0).astype(np.float64)
    half = ROPE_AXIS_DIM // 2
    inv = 1.0 / (ROPE_BASE ** (np.arange(0, ROPE_AXIS_DIM, 2, dtype=np.float64) / ROPE_AXIS_DIM))
    lane = np.arange(LANES)
    d = lane % QK_DIM
    axis = d // ROPE_AXIS_DIM
    freq = d % half
    ang = pos[axis, :].T * inv[freq][None, :]
    first = ((d % ROPE_AXIS_DIM) < half)[None, :]
    cos = np.cos(ang)
    sin = np.sin(ang)
    sin_next = np.where(first, -sin, 0.0)
    sin_prev = np.where(first, 0.0, sin)
    return (cos.astype(np.float32), sin_next.astype(np.float32), sin_prev.astype(np.float32))


@functools.lru_cache(maxsize=None)
def _dft_tables(seq_len):
    def cs(n):
        k = np.arange(n)
        m = (k[:, None] * k[None, :]) % n
        a = 2.0 * np.pi * m.astype(np.float64) / n
        return np.cos(a), np.sin(a)
    cl, sl = cs(seq_len)
    cc, sc = cs(FOURIER_HEAD_DIM)
    pos = np.concatenate([cl, -sl], axis=1).astype(np.float32)
    ch = np.concatenate([cc, sc], axis=1).astype(np.float32)
    return pos, ch


@functools.lru_cache(maxsize=None)
def _group_mean_matrix():
    g = np.arange(MIXER_WIDTH // 2) // QK_DIM
    p = (g[:, None] == g[None, :]).astype(np.float32) / QK_DIM
    return jnp.asarray(p, dtype=BF16)


def _mod_kernel(c_ref, w_ref, b_ref, o_ref):
    c = c_ref[...]
    s = (c * jax.nn.sigmoid(c)).astype(BF16)
    o_ref[...] = jnp.dot(s, w_ref[...].astype(BF16), preferred_element_type=F32) + b_ref[...]


def _modulation(cc, w_ada, b_ada):
    tn = 1024
    n_cols = 6 * D_MODEL
    return pl.pallas_call(
        _mod_kernel,
        grid=(DEPTH, n_cols // tn),
        in_specs=[
            pl.BlockSpec((8, D_MODEL), lambda l, j: (0, 0)),
            pl.BlockSpec((None, D_MODEL, tn), lambda l, j: (l, 0, j)),
            pl.BlockSpec((None, 1, tn), lambda l, j: (l, 0, j)),
        ],
        out_specs=pl.BlockSpec((None, 8, tn), lambda l, j: (l, 0, j)),
        out_shape=jax.ShapeDtypeStruct((DEPTH, 8, n_cols), F32),
        compiler_params=_params("arbitrary", "arbitrary"),
        name="adaln_mod",
    )(cc, w_ada, b_ada.reshape(DEPTH, 1, n_cols))


def _mod_spec(layer, chunk, row_of_block):
    return pl.BlockSpec((None, None, 1, D_MODEL),
                        lambda i, *_: (layer, row_of_block(i), 0, chunk))


def _rms(x, gain):
    return x * lax.rsqrt(jnp.mean(x * x, axis=-1, keepdims=True) + EPS) * gain


def _rope(t, cos, sin_next, sin_prev):
    outs = []
    for c in range(MIXER_WIDTH // LANES):
        tc = t[:, c * LANES:(c + 1) * LANES]
        outs.append(tc * cos
                    + pltpu.roll(tc, LANES - ROPE_AXIS_DIM // 2, axis=1) * sin_next
                    + pltpu.roll(tc, ROPE_AXIS_DIM // 2, axis=1) * sin_prev)
    return jnp.concatenate(outs, axis=1)


def _inproj_kernel(latent, chunks_per_seq, *refs):
    xp_ref, x_ref, xn_ref, sh_ref, sc_ref, g1_ref, w_ref, gq_ref, gk_ref, p_ref = refs[:10]
    n_in = 10
    if latent:
        cos_ref, sn_ref, sp_ref = refs[n_in:n_in + 3]
        n_in += 3
    mix_refs = refs[n_in:n_in + 8]
    n_in += 8
    ypool_ref, q_ref, k_ref, v_ref, yconv_ref, ufour_ref = refs[n_in:n_in + 6]
    n_out = 6
    if not latent:
        kf_ref, vf_ref = refs[n_in + 6:n_in + 8]
        n_out = 8
    h_scr, up_scr, glu_scr, pool_scr, conv_scr, shift_scr, acc_scr = refs[n_in + n_out:]
    w = MIXER_WIDTH
    tm = x_ref.shape[0]
    own = slice(HALO, HALO + tm)

    def mod_norm(x):
        return (_rms(x, g1_ref[...]) * (1.0 + sc_ref[...]) + sh_ref[...]).astype(BF16)

    h_scr[0:HALO, :] = mod_norm(xp_ref[...])
    h_scr[own, :] = mod_norm(x_ref[...])
    h_scr[HALO + tm:, :] = mod_norm(xn_ref[...])

    def proj(n, rows=own):
        return jnp.dot(h_scr[rows, :], w_ref[:, n * w:(n + 1) * w], preferred_element_type=F32)

    def qk_norm(z, gain_ref):
        zz = (z * z).astype(BF16)
        half = p_ref.shape[0]
        ms = jnp.concatenate([jnp.dot(zz[:, :half], p_ref[...], preferred_element_type=F32),
                              jnp.dot(zz[:, half:], p_ref[...], preferred_element_type=F32)], axis=1)
        return z * lax.rsqrt(ms + EPS) * gain_ref[...]

    everything = slice(0, tm + 2 * HALO)
    up_scr[...] = proj(0, everything)
    glu_scr[...] = proj(4, everything) * jax.nn.sigmoid(proj(5, everything))

    qn = qk_norm(proj(1), gq_ref)
    if latent:
        qn = _rope(qn, cos_ref[...], sn_ref[...], sp_ref[...])
    q_ref[...] = (qn * (QK_DIM ** -0.5 * LOG2E)).astype(BF16)

    kn = qk_norm(proj(2), gk_ref)
    if latent:
        kn = _rope(kn, cos_ref[...], sn_ref[...], sp_ref[...])
    else:
        kf_ref[...] = kn
    k_ref[...] = kn.astype(BF16)

    v = proj(3)
    if not latent:
        vf_ref[...] = v
    v_ref[...] = v.astype(BF16)
    ufour_ref[...] = proj(6).astype(BF16)

    chunks = tm // SEQ_CHUNK
    for c in range(chunks):
        lo = c * SEQ_CHUNK
        views = [scr.at[rows, :] for scr in (up_scr, glu_scr)
                 for rows in (slice(lo, lo + HALO), slice(lo + HALO, lo + HALO + SEQ_CHUNK),
                              slice(lo + HALO + SEQ_CHUNK, lo + 2 * HALO + SEQ_CHUNK))]
        y_pool, y_conv = _seqmix_chunk(pl.program_id(0) * chunks + c, chunks_per_seq, *views, *mix_refs,
                                       pool_scr, conv_scr, shift_scr, acc_scr)
        ypool_ref[lo:lo + SEQ_CHUNK, :] = y_pool
        yconv_ref[lo:lo + SEQ_CHUNK, :] = y_conv


def _inproj(x, mod, g_norm1, w_in, g_q, g_k, p, layer, seq_len, latent, tm):
    m_rows = x.shape[0]
    w = MIXER_WIDTH
    if latent:
        row_of = lambda i: 1 + i // (seq_len // tm)
    else:
        row_of = lambda i: 0
    hb = tm // HALO
    n_halo_blocks = m_rows // HALO
    row_spec = lambda: pl.BlockSpec((tm, w), lambda i: (i, 0))
    vec_spec = lambda: pl.BlockSpec((1, w), lambda i: (0, 0))
    layer_vec = lambda a: a.reshape(DEPTH, 1, a.shape[-1])
    layer_vec_spec = lambda n=w: pl.BlockSpec((None, 1, n), lambda i: (layer, 0, 0))
    in_specs = [
        pl.BlockSpec((HALO, D_MODEL), lambda i: (jnp.maximum(i * hb - 1, 0), 0)),
        pl.BlockSpec((tm, D_MODEL), lambda i: (i, 0)),
        pl.BlockSpec((HALO, D_MODEL), lambda i: (jnp.minimum((i + 1) * hb, n_halo_blocks - 1), 0)),
        _mod_spec(layer, 0, row_of),
        _mod_spec(layer, 1, row_of),
        layer_vec_spec(D_MODEL),
        pl.BlockSpec((None, D_MODEL, IN_COLS), lambda i: (layer, 0, 0), pipeline_mode=pl.Buffered(1)),
        vec_spec(), vec_spec(),
        pl.BlockSpec((w // 2, w // 2), lambda i: (0, 0)),
    ]
    args = [x, x, x, mod, mod, layer_vec(g_norm1), w_in,
            jnp.tile(g_q[layer], w // QK_DIM).reshape(1, w),
            jnp.tile(g_k[layer], w // QK_DIM).reshape(1, w),
            _group_mean_matrix()]
    if latent:
        blocks_per_seq = seq_len // tm
        for tab in _rope_tables(seq_len):
            in_specs.append(pl.BlockSpec((tm, LANES), lambda i: (i % blocks_per_seq, 0)))
            args.append(jnp.asarray(tab))
    in_specs += [pl.BlockSpec((None, len(POOL_WINDOWS), POOL_GROUP_DIM, POOL_GROUP_DIM),
                              lambda i: (layer, 0, 0, 0)),
                 layer_vec_spec(),
                 pl.BlockSpec((None, CONV_TAPS, SUBLANES, w), lambda i: (layer, 0, 0, 0)),
                 layer_vec_spec(), layer_vec_spec(), layer_vec_spec(),
                 pl.BlockSpec((None, w, w), lambda i: (layer, 0, 0)),
                 layer_vec_spec()]
    args += [p["pool_w"], layer_vec(p["pool_scale"]),
             jnp.broadcast_to(p["conv_dw"][:, :, None, :], (DEPTH, CONV_TAPS, SUBLANES, w)),
             layer_vec(p["conv_dw_b"]), layer_vec(p["conv_ln_g"]), layer_vec(p["conv_ln_b"]),
             p["conv_pw"], layer_vec(p["conv_pw_b"])]
    out_shape = [jax.ShapeDtypeStruct((m_rows, w), BF16)] * 6
    out_specs = [row_spec() for _ in range(6)]
    if not latent:
        out_shape += [jax.ShapeDtypeStruct((m_rows, w), F32)] * 2
        out_specs += [row_spec(), row_spec()]
    r = SEQ_CHUNK
    return pl.pallas_call(
        functools.partial(_inproj_kernel, latent, seq_len // r),
        grid=(m_rows // tm,),
        in_specs=in_specs,
        out_specs=out_specs,
        out_shape=out_shape,
        scratch_shapes=[pltpu.VMEM((tm + 2 * HALO, D_MODEL), BF16),
                        pltpu.VMEM((tm + 2 * HALO, w), F32), pltpu.VMEM((tm + 2 * HALO, w), F32),
                        pltpu.VMEM((r + 2 * HALO, w), F32), pltpu.VMEM((r + 2 * HALO, w), F32),
                        pltpu.VMEM((SUBLANES - 1, r + SUBLANES * (CONV_TAPS // SUBLANES), w), F32),
                        pltpu.VMEM((r, w), F32)],
        compiler_params=_params("arbitrary"),
        name="inproj_lat" if latent else "inproj_ctx",
    )(*args)


def _shift_rows(x, s):
    return pltpu.roll(x, s % x.shape[0], axis=0)


def _seqmix_chunk(i, chunks_per_seq, up_prev, up_cur, up_next, gl_prev, gl_cur, gl_next,
                  pw_ref, ps_ref, dw_ref, dwb_ref, lng_ref, lnb_ref, cw_ref, cwb_ref,
                  pool_scr, conv_scr, shift_scr, acc_scr):
    r = SEQ_CHUNK
    first = (i % chunks_per_seq) == 0
    last = (i % chunks_per_seq) == chunks_per_seq - 1
    zero_halo = jnp.zeros((HALO, MIXER_WIDTH), F32)

    def fill(scr, prev, cur, nxt):
        scr[0:HALO, :] = jnp.where(first, zero_halo, prev[...])
        scr[HALO:HALO + r, :] = cur[...]
        scr[HALO + r:2 * HALO + r, :] = jnp.where(last, zero_halo, nxt[...])

    fill(pool_scr, up_prev, up_cur, up_next)
    fill(conv_scr, gl_prev, gl_cur, gl_next)

    row = lax.broadcasted_iota(jnp.int32, (r, 1), 0) + (i % chunks_per_seq) * r
    seq_len = chunks_per_seq * r
    outs = []
    for g, win in enumerate(POOL_WINDOWS):
        lanes = slice(g * POOL_GROUP_DIM, (g + 1) * POOL_GROUP_DIM)
        u = pool_scr[:, lanes]
        s = _shift_rows(u, 1) + u
        span = 2
        while span < win:
            s = _shift_rows(s, span // 2) + _shift_rows(s, -(span // 2))
            span *= 2
        lo = jnp.maximum(row - win // 2, 0)
        hi = jnp.minimum(row + win // 2 - 1, seq_len - 1)
        cnt = (hi - lo + 1).astype(F32)
        p = s[HALO:HALO + r] / cnt - u[HALO:HALO + r]
        outs.append(jnp.dot(p.astype(BF16), pw_ref[g], preferred_element_type=F32))
    y_pool = (jnp.concatenate(outs, axis=1) * ps_ref[...]).astype(BF16)

    shifted_rows = r + SUBLANES * (CONV_TAPS // SUBLANES)
    for lo in range(1, SUBLANES):
        shift_scr[lo - 1] = conv_scr[lo:lo + shifted_rows, :]
    groups = CONV_SUB // SUBLANES
    for c in range(r // CONV_SUB):
        acc = jnp.zeros((groups, SUBLANES, MIXER_WIDTH), F32)
        for j in range(CONV_TAPS):
            hi, lo = divmod(j + 1, SUBLANES)
            src = conv_scr if lo == 0 else shift_scr.at[lo - 1]
            start = c * CONV_SUB + SUBLANES * hi
            rows = src[start:start + CONV_SUB, :].reshape(groups, SUBLANES, MIXER_WIDTH)
            acc = acc + rows * dw_ref[j]
        acc_scr[c * CONV_SUB:(c + 1) * CONV_SUB, :] = acc.reshape(CONV_SUB, MIXER_WIDTH)
    y = acc_scr[...] + dwb_ref[...]
    mu = jnp.mean(y, axis=-1, keepdims=True)
    yc = y - mu
    yn = yc * lax.rsqrt(jnp.mean(yc * yc, axis=-1, keepdims=True) + EPS) * lng_ref[...] + lnb_ref[...]
    act = yn * jax.nn.sigmoid(yn)
    y_conv = (jnp.dot(act.astype(BF16), cw_ref[...], preferred_element_type=F32)
              + cwb_ref[...]).astype(BF16)
    return y_pool, y_conv


def _outproj_kernel(x_ref, p0, p1, p2, p3, w_ref, g1_ref, sh_ref, sc_ref, gn_ref, x1_ref, h2_ref):
    mix = jnp.concatenate([p0[...], p1[...], p2[...], p3[...]], axis=1)
    a = jnp.dot(mix, w_ref[...], preferred_element_type=F32)
    x1 = x_ref[...] + g1_ref[...] * a
    x1_ref[...] = x1
    h2_ref[...] = (_rms(x1, gn_ref[...]) * (1.0 + sc_ref[...]) + sh_ref[...]).astype(BF16)


def _outproj(x, parts, w_out, mod, g_norm2, layer, row_of, tm, name):
    m_rows = x.shape[0]
    w = MIXER_WIDTH
    row_spec = lambda: pl.BlockSpec((tm, D_MODEL), lambda i: (i, 0))
    part_spec = lambda: pl.BlockSpec((tm, w), lambda i: (i, 0))
    return pl.pallas_call(
        _outproj_kernel,
        grid=(m_rows // tm,),
        in_specs=[row_spec(), part_spec(), part_spec(), part_spec(), part_spec(),
                  pl.BlockSpec((None, D_MODEL, D_MODEL), lambda i: (layer, 0, 0), pipeline_mode=pl.Buffered(1)),
                  _mod_spec(layer, 2, row_of), _mod_spec(layer, 3, row_of), _mod_spec(layer, 4, row_of),
                  pl.BlockSpec((None, 1, D_MODEL), lambda i: (layer, 0, 0))],
        out_specs=[row_spec(), row_spec()],
        out_shape=[jax.ShapeDtypeStruct((m_rows, D_MODEL), F32),
                   jax.ShapeDtypeStruct((m_rows, D_MODEL), BF16)],
        compiler_params=_params("arbitrary"),
        name=name,
    )(x, *parts, w_out, mod, mod, mod, g_norm2.reshape(DEPTH, 1, D_MODEL))


def _fourier_kernel(seq_len, scale, u_ref, pos_ref, ch_ref, fw_ref, o_ref, ab_scr, pos_scr):
    b = pl.program_id(1)

    @pl.when(b == 0)
    def _():
        pos_scr[...] = pos_ref[...].astype(BF16)

    @pl.when(pl.program_id(0) == 0)
    def _():
        ch = ch_ref[...].astype(BF16)
        for h in range(MIXER_WIDTH // FOURIER_HEAD_DIM):
            lanes = slice(h * FOURIER_HEAD_DIM, (h + 1) * FOURIER_HEAD_DIM)
            a = jnp.dot(u_ref[:, lanes], ch, preferred_element_type=F32)
            ab_scr[b, 0:seq_len, lanes] = a[:, :FOURIER_HEAD_DIM].astype(BF16)
            ab_scr[b, seq_len:2 * seq_len, lanes] = a[:, FOURIER_HEAD_DIM:].astype(BF16)

    f = jnp.dot(pos_scr[...], ab_scr[b], preferred_element_type=F32) * scale
    o_ref[...] = jnp.dot(f.astype(BF16), fw_ref[...], preferred_element_type=F32).astype(BF16)


def _fourier(ufour, fourier_w, layer, seq_len, name):
    m_rows = ufour.shape[0]
    w = MIXER_WIDTH
    batch = m_rows // seq_len
    tl = min(seq_len, 256)
    blocks = seq_len // tl
    pos_tab, ch_tab = _dft_tables(seq_len)
    scale = float(1.0 / np.sqrt(seq_len * FOURIER_HEAD_DIM))
    return pl.pallas_call(
        functools.partial(_fourier_kernel, seq_len, scale),
        grid=(blocks, batch),
        in_specs=[pl.BlockSpec((seq_len, w), lambda j, b: (jnp.where(j == 0, b, batch - 1), 0)),
                  pl.BlockSpec((tl, 2 * seq_len), lambda j, b: (j, 0)),
                  pl.BlockSpec((FOURIER_HEAD_DIM, 2 * FOURIER_HEAD_DIM), lambda j, b: (0, 0)),
                  pl.BlockSpec((None, w, w), lambda j, b: (layer, 0, 0))],
        out_specs=pl.BlockSpec((tl, w), lambda j, b: (b * blocks + j, 0)),
        out_shape=jax.ShapeDtypeStruct((m_rows, w), BF16),
        scratch_shapes=[pltpu.VMEM((batch, 2 * seq_len, w), BF16), pltpu.VMEM((tl, 2 * seq_len), BF16)],
        compiler_params=_params("arbitrary", "arbitrary"),
        name=name,
    )(ufour, pos_tab, ch_tab, fourier_w)


def _cast_blocks(refs):
    half = len(refs) // 2
    for src, dst in zip(refs[:half], refs[half:]):
        if len(dst.shape) == len(src.shape):
            dst[...] = src[...].astype(BF16)
        else:
            for j in range(dst.shape[0]):
                dst[j] = src[:, j * FFN_TF:(j + 1) * FFN_TF].astype(BF16)


def _ffn_weight_cast_specs(weights, layer, n_row, n_col, row_col_of):
    br, bc = D_MODEL // n_row, D_FF // n_col
    slabs = bc // FFN_TF

    def up_map(*ids):
        r, c = row_col_of(*ids)
        return (layer, r, c)

    def down_map(*ids):
        r, c = row_col_of(*ids)
        return (layer, c, r)

    def tiled_map(*ids):
        r, c = row_col_of(*ids)
        return (0, c, r, 0)

    in_specs = [pl.BlockSpec((None, br, bc), up_map), pl.BlockSpec((None, br, bc), up_map),
                pl.BlockSpec((None, bc, br), down_map)]
    out_specs = [pl.BlockSpec((None, slabs, br, FFN_TF), tiled_map),
                 pl.BlockSpec((None, slabs, br, FFN_TF), tiled_map),
                 pl.BlockSpec((None, bc, br), lambda *ids: (0,) + down_map(*ids)[1:])]
    tiled = jax.ShapeDtypeStruct((1, D_FF // FFN_TF, D_MODEL, FFN_TF), BF16)
    out_shape = [tiled, tiled, jax.ShapeDtypeStruct((1, D_FF, D_MODEL), BF16)]
    return in_specs, out_specs, out_shape


def _attn_kernel(lam_init, cached, heads, n_cast, *refs):
    n_in = (7 if cached else 5) + n_cast
    cast_src, cast_dst = refs[n_in - n_cast:n_in], refs[n_in + 1:n_in + 1 + n_cast]
    o_ref = refs[n_in]
    scratch = refs[n_in + 1 + n_cast:]
    if cached:
        q_ref, k_ref, v_ref, ck_ref, cv_ref, lam_ref, gs_ref = refs[:7]
        vt_scr, s_scr, ckb_scr = scratch
    else:
        q_ref, k_ref, v_ref, lam_ref, gs_ref = refs[:5]
        vt_scr, s_scr = scratch
    _cast_blocks(cast_src + cast_dst)
    tq = q_ref.shape[0]
    own = k_ref.shape[0]
    past = ck_ref.shape[0] if cached else 0

    @pl.when(pl.program_id(2) == 0)
    def _():
        for h in range(heads):
            cols = slice(h * V_DIM, (h + 1) * V_DIM)
            base = h * VT_ROWS
            vt_scr[base:base + V_DIM, 0:own] = v_ref[:, cols].astype(F32).T.astype(BF16)
            vt_scr[base + V_DIM:base + VT_ROWS, :] = jnp.ones((VT_ROWS - V_DIM, own + past), BF16)
            if cached:
                vt_scr[base:base + V_DIM, own:] = cv_ref[:, cols].T.astype(BF16)
                ckb_scr[:, cols] = ck_ref[:, cols].astype(BF16)

    lv = lam_ref[...]
    lam = (jnp.exp(jnp.sum(lv[0:1] * lv[1:2], axis=-1, keepdims=True))
           - jnp.exp(jnp.sum(lv[2:3] * lv[3:4], axis=-1, keepdims=True)) + lam_init)
    lane = lax.broadcasted_iota(jnp.int32, (1, V_DIM), 1)
    nt = (((1,), (1,)), ((), ()))
    segments = [(k_ref, 0, own)] + ([(ckb_scr, own, past)] if cached else [])
    items = [(h, key_ref, offset, c0, min(KEY_CHUNK, count - c0))
             for h in range(heads) for key_ref, offset, count in segments
             for c0 in range(0, count, KEY_CHUNK)]
    qqs = []
    for h in range(heads):
        q = q_ref[:, h * V_DIM:(h + 1) * V_DIM]
        zero = jnp.zeros_like(q)
        qqs.append(jnp.concatenate([jnp.where(lane < QK_DIM, q, zero),
                                    jnp.where(lane >= QK_DIM, q, zero)], axis=0))

    def scores(n):
        h, key_ref, _, c0, kc = items[n]
        s_scr[n % 2, 0:kc, :] = lax.dot_general(key_ref[c0:c0 + kc, h * V_DIM:(h + 1) * V_DIM], qqs[h], nt,
                                                preferred_element_type=F32)

    scores(0)
    parts = [[] for _ in range(heads)]
    for n, (h, _, offset, c0, kc) in enumerate(items):
        if n + 1 < len(items):
            scores(n + 1)
        s = s_scr[n % 2, 0:kc, :]
        mc = s.max(axis=0, keepdims=True)
        e = jnp.exp2(s - mc).astype(BF16)
        oc = jnp.dot(vt_scr[h * VT_ROWS:(h + 1) * VT_ROWS, offset + c0:offset + c0 + kc], e,
                     preferred_element_type=F32)
        parts[h].append((mc, oc))
    for h in range(heads):
        m = functools.reduce(jnp.maximum, [mc for mc, _ in parts[h]])
        acc = functools.reduce(jnp.add, [oc * jnp.exp2(mc - m) for mc, oc in parts[h]])
        inv = 1.0 / acc[V_DIM:V_DIM + 1]
        o_t = acc[:V_DIM, :tq] * inv[:, :tq] - lam * (acc[:V_DIM, tq:] * inv[:, tq:])
        o_ref[:, h * V_DIM:(h + 1) * V_DIM] = (_rms(o_t.T, gs_ref[...]) * (1.0 - lam_init)).astype(BF16)


def _attention(q, k, v, cache_k, cache_v, lam, g_subln, layer, lam_init, seq_len, tq, heads, cast, name):
    m_rows = q.shape[0]
    batch = m_rows // seq_len
    qb = seq_len // tq
    cached = cache_k is not None
    hw = heads * V_DIM
    in_specs = [pl.BlockSpec((tq, hw), lambda b, h, i: (b * qb + i, h)),
                pl.BlockSpec((seq_len, hw), lambda b, h, i: (b, h)),
                pl.BlockSpec((seq_len, hw), lambda b, h, i: (b, h))]
    args = [q, k, v]
    past = 0
    scratch = []
    if cached:
        past = cache_k.shape[2]
        spec = lambda: pl.BlockSpec((None, None, past, hw), lambda b, h, i: (b, layer, 0, h))
        in_specs += [spec(), spec()]
        args += [cache_k, cache_v]
        scratch = [pltpu.VMEM((past, hw), BF16)]
    in_specs += [pl.BlockSpec((None, 4, QK_DIM), lambda b, h, i: (layer, 0, 0)),
                 pl.BlockSpec((None, 1, V_DIM), lambda b, h, i: (layer, 0, 0))]
    args += [lam, g_subln.reshape(DEPTH, 1, V_DIM)]
    groups = N_HEADS // heads
    out_specs = [pl.BlockSpec((tq, hw), lambda b, h, i: (b * qb + i, h))]
    out_shape = [jax.ShapeDtypeStruct((m_rows, N_HEADS * V_DIM), BF16)]
    n_cast = 0
    if cast is not None:
        weights, cast_layer = cast
        assert groups == 1
        ci, co, cs = _ffn_weight_cast_specs(weights, cast_layer, batch * qb, 1,
                                            lambda b, h, i: (b * qb + i, 0))
        in_specs += ci
        out_specs += co
        out_shape += cs
        args += list(weights)
        n_cast = len(weights)
    outs = pl.pallas_call(
        functools.partial(_attn_kernel, lam_init, cached, heads, n_cast),
        grid=(batch, groups, qb),
        in_specs=in_specs,
        out_specs=out_specs,
        out_shape=out_shape,
        scratch_shapes=[pltpu.VMEM((heads * VT_ROWS, seq_len + past), BF16),
                        pltpu.VMEM((2, min(KEY_CHUNK, seq_len), 2 * tq), F32)] + scratch,
        compiler_params=_params("arbitrary", "arbitrary", "arbitrary"),
        name=name,
    )(*args)
    return outs[0], tuple(outs[1:])


def _ffn_kernel(n_cast, h_ref, x_ref, wg_ref, wu_ref, wd_ref, g2_ref, *rest):
    cast_src, o_ref, cast_dst, acc_scr = rest[:n_cast], rest[n_cast], rest[n_cast + 1:-1], rest[-1]
    f = pl.program_id(1)
    last = pl.num_programs(1) - 1
    _cast_blocks(cast_src + cast_dst)

    def partial_down():
        h = h_ref[...]
        g = jnp.dot(h, wg_ref[...], preferred_element_type=F32)
        u = jnp.dot(h, wu_ref[...], preferred_element_type=F32)
        a = (g * jax.nn.sigmoid(g) * u).astype(BF16)
        return jnp.dot(a, wd_ref[...], preferred_element_type=F32)

    @pl.when(f == 0)
    def _():
        acc_scr[...] = partial_down()

    @pl.when((f > 0) & (f < last))
    def _():
        acc_scr[...] += partial_down()

    @pl.when(f == last)
    def _():
        o_ref[...] = x_ref[...] + g2_ref[...] * (acc_scr[...] + partial_down())


def _ffn(h2, x1, ffn_w, wl, mod, layer, row_of, tm, cast, name):
    tf = FFN_TF
    m_rows = x1.shape[0]
    grid = (m_rows // tm, D_FF // tf)
    in_specs = [pl.BlockSpec((tm, D_MODEL), lambda i, f: (i, 0)),
                pl.BlockSpec((tm, D_MODEL), lambda i, f: (i, 0)),
                pl.BlockSpec((None, None, D_MODEL, tf), lambda i, f: (wl, f, 0, 0)),
                pl.BlockSpec((None, None, D_MODEL, tf), lambda i, f: (wl, f, 0, 0)),
                pl.BlockSpec((None, tf, D_MODEL), lambda i, f: (wl, f, 0)),
                _mod_spec(layer, 5, row_of)]
    out_specs = [pl.BlockSpec((tm, D_MODEL), lambda i, f: (i, 0))]
    out_shape = [jax.ShapeDtypeStruct((m_rows, D_MODEL), F32)]
    args = [h2, x1, *ffn_w, mod]
    n_cast = 0
    if cast is not None:
        weights, cast_layer = cast
        ci, co, cs = _ffn_weight_cast_specs(weights, cast_layer, grid[0], grid[1], lambda i, f: (i, f))
        in_specs += ci
        out_specs += co
        out_shape += cs
        args += list(weights)
        n_cast = len(weights)
    outs = pl.pallas_call(
        functools.partial(_ffn_kernel, n_cast),
        grid=grid,
        in_specs=in_specs,
        out_specs=out_specs,
        out_shape=out_shape,
        scratch_shapes=[pltpu.VMEM((tm, D_MODEL), F32)],
        compiler_params=_params("arbitrary", "arbitrary"),
        name=name,
    )(*args)
    return outs[0], tuple(outs[1:])


def kernel(x_prompt, x_sample, cache_k, cache_v, c, c_ctx, w_ada, b_ada, g_norm1, w_in, pool_w, pool_scale, g_q, g_k, lam, g_subln, conv_dw, conv_dw_b, conv_ln_g, conv_ln_b, conv_pw, conv_pw_b, fourier_w, w_out, g_norm2, w_gate, w_up, w_down):
    batch, seq, _ = x_prompt.shape
    dec_batch, dec_seq, _ = x_sample.shape
    past = cache_k.shape[2]
    tm = 512

    cc = jnp.concatenate([c_ctx[None, :], c, jnp.zeros((8 - 1 - dec_batch, D_MODEL), F32)], axis=0)
    mod = _modulation(cc, w_ada, b_ada).reshape(DEPTH, 8, 1, 6 * D_MODEL)

    bf = lambda a: a.astype(BF16)
    w_in_b, w_out_b, fourier_w_b = bf(w_in), bf(w_out), bf(fourier_w)
    mix_params = dict(pool_w=bf(pool_w), pool_scale=pool_scale, conv_dw=conv_dw, conv_dw_b=conv_dw_b,
                      conv_ln_g=conv_ln_g, conv_ln_b=conv_ln_b, conv_pw=bf(conv_pw), conv_pw_b=conv_pw_b)
    ffn_f32 = (w_gate, w_up, w_down)
    ck = cache_k.reshape(dec_batch, DEPTH, past, N_HEADS * V_DIM)
    cv = cache_v.reshape(dec_batch, DEPTH, past, N_HEADS * V_DIM)

    streams = {
        "lat": dict(x=x_sample.reshape(dec_batch * dec_seq, D_MODEL), seq=dec_seq, latent=True,
                    row_of=lambda i: 1 + i // (dec_seq // tm)),
        "ctx": dict(x=x_prompt.reshape(batch * seq, D_MODEL), seq=seq, latent=False,
                    row_of=lambda i: 0),
    }
    new_k, new_v = [], []
    ffn_b = None
    ffn_next = None
    for layer in range(DEPTH):
        lam_init = 0.8 - 0.6 * float(np.exp(-0.3 * layer))
        if layer > 0:
            ffn_b = ffn_next
        for name, st in streams.items():
            latent, seq_len = st["latent"], st["seq"]
            outs = _inproj(st["x"], mod, g_norm1, w_in_b, g_q, g_k, mix_params, layer, seq_len, latent, tm)
            y_pool, q, k, v, y_conv, ufour = outs[:6]
            if not latent:
                new_k.append(outs[6].reshape(batch, seq, N_HEADS, 2, QK_DIM))
                new_v.append(outs[7].reshape(batch, seq, N_HEADS, V_DIM))
            y_four = _fourier(ufour, fourier_w_b, layer, seq_len, "fourier_" + name)
            att, cast_out = _attention(q, k, v, ck if latent else None, cv if latent else None, lam, g_subln,
                                       layer, lam_init, seq_len, 512 if latent else seq_len, N_HEADS,
                                       (ffn_f32, 0) if (latent and layer == 0) else None, "attn_" + name)
            if cast_out:
                ffn_b = cast_out
            x1, h2 = _outproj(st["x"], (y_pool, att, y_conv, y_four), w_out_b, mod, g_norm2, layer,
                              st["row_of"], tm, "outproj_" + name)
            cast = (ffn_f32, layer + 1) if (latent and layer + 1 < DEPTH) else None
            st["x"], cast_out = _ffn(h2, x1, ffn_b, 0, mod, layer, st["row_of"], tm, cast, "ffn_" + name)
            if cast_out:
                ffn_next = cast_out
    y_prompt = streams["ctx"]["x"].reshape(batch, seq, D_MODEL)
    y_sample = streams["lat"]["x"].reshape(dec_batch, dec_seq, D_MODEL)
    return (y_prompt, y_sample, jnp.stack(new_k, axis=1), jnp.stack(new_v, axis=1))
```

```python
import functools

import numpy as np
import jax
import jax.numpy as jnp
from jax import lax
from jax.experimental import pallas as pl
from jax.experimental.pallas import tpu as pltpu

D_MODEL = 2048
DEPTH = 2
GRID_W = 64
MIXER_WIDTH = 512
POOL_GROUP_DIM = 128
POOL_WINDOWS = (2, 4, 8, 16)
N_HEADS = 4
V_DIM = 128
QK_DIM = 64
ROPE_AXIS_DIM = 32
ROPE_BASE = 10000.0
CONV_TAPS = 31
FOURIER_HEAD_DIM = 128
IN_COLS = 7 * MIXER_WIDTH
D_FF = 5632
EPS = 1e-6

LANES = 128
SUBLANES = 8
HALO = 16
SEQ_CHUNK = 256
CONV_SUB = 32
KEY_CHUNK = 512
FFN_TF = 512
VT_ROWS = V_DIM + 2 * SUBLANES
LOG2E = 1.4426950408889634
VMEM_LIMIT = 56 * 1024 * 1024

BF16 = jnp.bfloat16
F32 = jnp.float32


def _params(*semantics):
    return pltpu.CompilerParams(dimension_semantics=semantics, vmem_limit_bytes=VMEM_LIMIT)


@functools.lru_cache(maxsize=None)
def _rope_tables(seq_len):
    t = np.arange(seq_len)
    pos = np.stack([(t // GRID_W), (t % GRID_W)], axis=0).astype(np.float64)
    half = ROPE_AXIS_DIM // 2
    inv = 1.0 / (ROPE_BASE ** (np.arange(0, ROPE_AXIS_DIM, 2, dtype=np.float64) / ROPE_AXIS_DIM))
    lane = np.arange(LANES)
    d = lane % QK_DIM
    axis = d // ROPE_AXIS_DIM
    freq = d % half
    ang = pos[axis, :].T * inv[freq][None, :]
    first = ((d % ROPE_AXIS_DIM) < half)[None, :]
    cos = np.cos(ang)
    sin = np.sin(ang)
    sin_next = np.where(first, -sin, 0.0)
    sin_prev = np.where(first, 0.0, sin)
    return (cos.astype(np.float32), sin_next.astype(np.float32), sin_prev.astype(np.float32))


@functools.lru_cache(maxsize=None)
def _dft_tables(seq_len):
    def cs(n):
        k = np.arange(n)
        m = (k[:, None] * k[None, :]) % n
        a = 2.0 * np.pi * m.astype(np.float64) / n
        return np.cos(a), np.sin(a)
    cl, sl = cs(seq_len)
    cc, sc = cs(FOURIER_HEAD_DIM)
    pos = np.concatenate([cl, -sl], axis=1).astype(np.float32)
    ch = np.concatenate([cc, sc], axis=1).astype(np.float32)
    return pos, ch


@functools.lru_cache(maxsize=None)
def _group_mean_matrix():
    g = np.arange(MIXER_WIDTH // 2) // QK_DIM
    p = (g[:, None] == g[None, :]).astype(np.float32) / QK_DIM
    return jnp.asarray(p, dtype=BF16)


def _mod_kernel(c_ref, w_ref, b_ref, o_ref):
    c = c_ref[...]
    s = (c * jax.nn.sigmoid(c)).astype(BF16)
    o_ref[...] = jnp.dot(s, w_ref[...].astype(BF16), preferred_element_type=F32) + b_ref[...]


def _modulation(cc, w_ada, b_ada):
    tn = 1024
    n_cols = 6 * D_MODEL
    return pl.pallas_call(
        _mod_kernel,
        grid=(DEPTH, n_cols // tn),
        in_specs=[
            pl.BlockSpec((8, D_MODEL), lambda l, j: (0, 0)),
            pl.BlockSpec((None, D_MODEL, tn), lambda l, j: (l, 0, j)),
            pl.BlockSpec((None, 1, tn), lambda l, j: (l, 0, j)),
        ],
        out_specs=pl.BlockSpec((None, 8, tn), lambda l, j: (l, 0, j)),
        out_shape=jax.ShapeDtypeStruct((DEPTH, 8, n_cols), F32),
        compiler_params=_params("arbitrary", "arbitrary"),
        name="adaln_mod",
    )(cc, w_ada, b_ada.reshape(DEPTH, 1, n_cols))


def _mod_spec(layer, chunk, row_of_block):
    return pl.BlockSpec((None, None, 1, D_MODEL),
                        lambda i, *_: (layer, row_of_block(i), 0, chunk))


def _rms(x, gain):
    return x * lax.rsqrt(jnp.mean(x * x, axis=-1, keepdims=True) + EPS) * gain


def _rope(t, cos, sin_next, sin_prev):
    outs = []
    for c in range(MIXER_WIDTH // LANES):
        tc = t[:, c * LANES:(c + 1) * LANES]
        outs.append(tc * cos
                    + pltpu.roll(tc, LANES - ROPE_AXIS_DIM // 2, axis=1) * sin_next
                    + pltpu.roll(tc, ROPE_AXIS_DIM // 2, axis=1) * sin_prev)
    return jnp.concatenate(outs, axis=1)


def _inproj_kernel(latent, chunks_per_seq, *refs):
    xp_ref, x_ref, xn_ref, sh_ref, sc_ref, g1_ref, w_ref, gq_ref, gk_ref, p_ref = refs[:10]
    n_in = 10
    if latent:
        cos_ref, sn_ref, sp_ref = refs[n_in:n_in + 3]
        n_in += 3
    mix_refs = refs[n_in:n_in + 8]
    n_in += 8
    ypool_ref, q_ref, k_ref, v_ref, yconv_ref, ufour_ref = refs[n_in:n_in + 6]
    n_out = 6
    if not latent:
        kf_ref, vf_ref = refs[n_in + 6:n_in + 8]
        n_out = 8
    h_scr, up_scr, glu_scr, pool_scr, conv_scr, shift_scr, acc_scr = refs[n_in + n_out:]
    w = MIXER_WIDTH
    tm = x_ref.shape[0]
    own = slice(HALO, HALO + tm)

    def mod_norm(x):
        return (_rms(x, g1_ref[...]) * (1.0 + sc_ref[...]) + sh_ref[...]).astype(BF16)

    h_scr[0:HALO, :] = mod_norm(xp_ref[...])
    h_scr[own, :] = mod_norm(x_ref[...])
    h_scr[HALO + tm:, :] = mod_norm(xn_ref[...])

    def proj(n, rows=own):
        return jnp.dot(h_scr[rows, :], w_ref[:, n * w:(n + 1) * w], preferred_element_type=F32)

    def qk_norm(z, gain_ref):
        zz = (z * z).astype(BF16)
        half = p_ref.shape[0]
        ms = jnp.concatenate([jnp.dot(zz[:, :half], p_ref[...], preferred_element_type=F32),
                              jnp.dot(zz[:, half:], p_ref[...], preferred_element_type=F32)], axis=1)
        return z * lax.rsqrt(ms + EPS) * gain_ref[...]

    everything = slice(0, tm + 2 * HALO)
    up_scr[...] = proj(0, everything)
    glu_scr[...] = proj(4, everything) * jax.nn.sigmoid(proj(5, everything))

    qn = qk_norm(proj(1), gq_ref)
    if latent:
        qn = _rope(qn, cos_ref[...], sn_ref[...], sp_ref[...])
    q_ref[...] = (qn * (QK_DIM ** -0.5 * LOG2E)).astype(BF16)

    kn = qk_norm(proj(2), gk_ref)
    if latent:
        kn = _rope(kn, cos_ref[...], sn_ref[...], sp_ref[...])
    else:
        kf_ref[...] = kn
    k_ref[...] = kn.astype(BF16)

    v = proj(3)
    if not latent:
        vf_ref[...] = v
    v_ref[...] = v.astype(BF16)
    ufour_ref[...] = proj(6).astype(BF16)

    chunks = tm // SEQ_CHUNK
    for c in range(chunks):
        lo = c * SEQ_CHUNK
        views = [scr.at[rows, :] for scr in (up_scr, glu_scr)
                 for rows in (slice(lo, lo + HALO), slice(lo + HALO, lo + HALO + SEQ_CHUNK),
                              slice(lo + HALO + SEQ_CHUNK, lo + 2 * HALO + SEQ_CHUNK))]
        y_pool, y_conv = _seqmix_chunk(pl.program_id(0) * chunks + c, chunks_per_seq, *views, *mix_refs,
                                       pool_scr, conv_scr, shift_scr, acc_scr)
        ypool_ref[lo:lo + SEQ_CHUNK, :] = y_pool
        yconv_ref[lo:lo + SEQ_CHUNK, :] = y_conv


def _inproj(x, mod, g_norm1, w_in, g_q, g_k, p, layer, seq_len, latent, tm):
    m_rows = x.shape[0]
    w = MIXER_WIDTH
    if latent:
        row_of = lambda i: 1 + i // (seq_len // tm)
    else:
        row_of = lambda i: 0
    hb = tm // HALO
    n_halo_blocks = m_rows // HALO
    row_spec = lambda: pl.BlockSpec((tm, w), lambda i: (i, 0))
    vec_spec = lambda: pl.BlockSpec((1, w), lambda i: (0, 0))
    layer_vec = lambda a: a.reshape(DEPTH, 1, a.shape[-1])
    layer_vec_spec = lambda n=w: pl.BlockSpec((None, 1, n), lambda i: (layer, 0, 0))
    in_specs = [
        pl.BlockSpec((HALO, D_MODEL), lambda i: (jnp.maximum(i * hb - 1, 0), 0)),
        pl.BlockSpec((tm, D_MODEL), lambda i: (i, 0)),
        pl.BlockSpec((HALO, D_MODEL), lambda i: (jnp.minimum((i + 1) * hb, n_halo_blocks - 1), 0)),
        _mod_spec(layer, 0, row_of),
        _mod_spec(layer, 1, row_of),
        layer_vec_spec(D_MODEL),
        pl.BlockSpec((None, D_MODEL, IN_COLS), lambda i: (0, 0, 0), pipeline_mode=pl.Buffered(1)),
        vec_spec(), vec_spec(),
        pl.BlockSpec((w // 2, w // 2), lambda i: (0, 0)),
    ]
    args = [x, x, x, mod, mod, layer_vec(g_norm1), w_in,
            jnp.tile(g_q[layer], w // QK_DIM).reshape(1, w),
            jnp.tile(g_k[layer], w // QK_DIM).reshape(1, w),
            _group_mean_matrix()]
    if latent:
        blocks_per_seq = seq_len // tm
        for tab in _rope_tables(seq_len):
            in_specs.append(pl.BlockSpec((tm, LANES), lambda i: (i % blocks_per_seq, 0)))
            args.append(jnp.asarray(tab))
    in_specs += [pl.BlockSpec((None, len(POOL_WINDOWS), POOL_GROUP_DIM, POOL_GROUP_DIM),
                              lambda i: (layer, 0, 0, 0)),
                 layer_vec_spec(),
                 pl.BlockSpec((None, CONV_TAPS, SUBLANES, w), lambda i: (layer, 0, 0, 0)),
                 layer_vec_spec(), layer_vec_spec(), layer_vec_spec(),
                 pl.BlockSpec((None, w, w), lambda i: (layer, 0, 0)),
                 layer_vec_spec()]
    args += [p["pool_w"], layer_vec(p["pool_scale"]),
             jnp.broadcast_to(p["conv_dw"][:, :, None, :], (DEPTH, CONV_TAPS, SUBLANES, w)),
             layer_vec(p["conv_dw_b"]), layer_vec(p["conv_ln_g"]), layer_vec(p["conv_ln_b"]),
             p["conv_pw"], layer_vec(p["conv_pw_b"])]
    out_shape = [jax.ShapeDtypeStruct((m_rows, w), BF16)] * 6
    out_specs = [row_spec() for _ in range(6)]
    if not latent:
        out_shape += [jax.ShapeDtypeStruct((m_rows, w), F32)] * 2
        out_specs += [row_spec(), row_spec()]
    r = SEQ_CHUNK
    return pl.pallas_call(
        functools.partial(_inproj_kernel, latent, seq_len // r),
        grid=(m_rows // tm,),
        in_specs=in_specs,
        out_specs=out_specs,
        out_shape=out_shape,
        scratch_shapes=[pltpu.VMEM((tm + 2 * HALO, D_MODEL), BF16),
                        pltpu.VMEM((tm + 2 * HALO, w), F32), pltpu.VMEM((tm + 2 * HALO, w), F32),
                        pltpu.VMEM((r + 2 * HALO, w), F32), pltpu.VMEM((r + 2 * HALO, w), F32),
                        pltpu.VMEM((SUBLANES - 1, r + SUBLANES * (CONV_TAPS // SUBLANES), w), F32),
                        pltpu.VMEM((r, w), F32)],
        compiler_params=_params("arbitrary"),
        name="inproj_lat" if latent else "inproj_ctx",
    )(*args)


def _shift_rows(x, s):
    return pltpu.roll(x, s % x.shape[0], axis=0)


def _seqmix_chunk(i, chunks_per_seq, up_prev, up_cur, up_next, gl_prev, gl_cur, gl_next,
                  pw_ref, ps_ref, dw_ref, dwb_ref, lng_ref, lnb_ref, cw_ref, cwb_ref,
                  pool_scr, conv_scr, shift_scr, acc_scr):
    r = SEQ_CHUNK
    first = (i % chunks_per_seq) == 0
    last = (i % chunks_per_seq) == chunks_per_seq - 1
    zero_halo = jnp.zeros((HALO, MIXER_WIDTH), F32)

    def fill(scr, prev, cur, nxt):
        scr[0:HALO, :] = jnp.where(first, zero_halo, prev[...])
        scr[HALO:HALO + r, :] = cur[...]
        scr[HALO + r:2 * HALO + r, :] = jnp.where(last, zero_halo, nxt[...])

    fill(pool_scr, up_prev, up_cur, up_next)
    fill(conv_scr, gl_prev, gl_cur, gl_next)

    row = lax.broadcasted_iota(jnp.int32, (r, 1), 0) + (i % chunks_per_seq) * r
    seq_len = chunks_per_seq * r
    outs = []
    for g, win in enumerate(POOL_WINDOWS):
        lanes = slice(g * POOL_GROUP_DIM, (g + 1) * POOL_GROUP_DIM)
        u = pool_scr[:, lanes]
        s = _shift_rows(u, 1) + u
        span = 2
        while span < win:
            s = _shift_rows(s, span // 2) + _shift_rows(s, -(span // 2))
            span *= 2
        lo = jnp.maximum(row - win // 2, 0)
        hi = jnp.minimum(row + win // 2 - 1, seq_len - 1)
        cnt = (hi - lo + 1).astype(F32)
        p = s[HALO:HALO + r] / cnt - u[HALO:HALO + r]
        outs.append(jnp.dot(p.astype(BF16), pw_ref[g], preferred_element_type=F32))
    y_pool = (jnp.concatenate(outs, axis=1) * ps_ref[...]).astype(BF16)

    shifted_rows = r + SUBLANES * (CONV_TAPS // SUBLANES)
    for lo in range(1, SUBLANES):
        shift_scr[lo - 1] = conv_scr[lo:lo + shifted_rows, :]
    groups = CONV_SUB // SUBLANES
    for c in range(r // CONV_SUB):
        acc = jnp.zeros((groups, SUBLANES, MIXER_WIDTH), F32)
        for j in range(CONV_TAPS):
            hi, lo = divmod(j + 1, SUBLANES)
            src = conv_scr if lo == 0 else shift_scr.at[lo - 1]
            start = c * CONV_SUB + SUBLANES * hi
            rows = src[start:start + CONV_SUB, :].reshape(groups, SUBLANES, MIXER_WIDTH)
            acc = acc + rows * dw_ref[j]
        acc_scr[c * CONV_SUB:(c + 1) * CONV_SUB, :] = acc.reshape(CONV_SUB, MIXER_WIDTH)
    y = acc_scr[...] + dwb_ref[...]
    mu = jnp.mean(y, axis=-1, keepdims=True)
    yc = y - mu
    yn = yc * lax.rsqrt(jnp.mean(yc * yc, axis=-1, keepdims=True) + EPS) * lng_ref[...] + lnb_ref[...]
    act = yn * jax.nn.sigmoid(yn)
    y_conv = (jnp.dot(act.astype(BF16), cw_ref[...], preferred_element_type=F32)
              + cwb_ref[...]).astype(BF16)
    return y_pool, y_conv


def _outproj_kernel(x_ref, p0, p1, p2, p3, w_ref, g1_ref, sh_ref, sc_ref, gn_ref, x1_ref, h2_ref):
    half = x_ref.shape[0] // 2
    for rows in (slice(0, half), slice(half, 2 * half)):
        mix = jnp.concatenate([p0[rows, :], p1[rows, :], p2[rows, :], p3[rows, :]], axis=1)
        a = jnp.dot(mix, w_ref[...], preferred_element_type=F32)
        x1 = x_ref[rows, :] + g1_ref[...] * a
        x1_ref[rows, :] = x1
        h2_ref[rows, :] = (_rms(x1, gn_ref[...]) * (1.0 + sc_ref[...]) + sh_ref[...]).astype(BF16)


def _outproj(x, parts, w_out, mod, g_norm2, layer, row_of, tm, name):
    m_rows = x.shape[0]
    w = MIXER_WIDTH
    row_spec = lambda: pl.BlockSpec((tm, D_MODEL), lambda i: (i, 0))
    part_spec = lambda: pl.BlockSpec((tm, w), lambda i: (i, 0))
    return pl.pallas_call(
        _outproj_kernel,
        grid=(m_rows // tm,),
        in_specs=[row_spec(), part_spec(), part_spec(), part_spec(), part_spec(),
                  pl.BlockSpec((None, D_MODEL, D_MODEL), lambda i: (0, 0, 0), pipeline_mode=pl.Buffered(1)),
                  _mod_spec(layer, 2, row_of), _mod_spec(layer, 3, row_of), _mod_spec(layer, 4, row_of),
                  pl.BlockSpec((None, 1, D_MODEL), lambda i: (layer, 0, 0))],
        out_specs=[row_spec(), row_spec()],
        out_shape=[jax.ShapeDtypeStruct((m_rows, D_MODEL), F32),
                   jax.ShapeDtypeStruct((m_rows, D_MODEL), BF16)],
        compiler_params=_params("arbitrary"),
        name=name,
    )(x, *parts, w_out, mod, mod, mod, g_norm2.reshape(DEPTH, 1, D_MODEL))


def _fourier_kernel(seq_len, scale, u_ref, pos_ref, ch_ref, fw_ref, o_ref, ab_scr, pos_scr):
    b = pl.program_id(1)

    @pl.when(b == 0)
    def _():
        pos_scr[...] = pos_ref[...].astype(BF16)

    @pl.when(pl.program_id(0) == 0)
    def _():
        ch = ch_ref[...].astype(BF16)
        for h in range(MIXER_WIDTH // FOURIER_HEAD_DIM):
            lanes = slice(h * FOURIER_HEAD_DIM, (h + 1) * FOURIER_HEAD_DIM)
            a = jnp.dot(u_ref[:, lanes], ch, preferred_element_type=F32)
            ab_scr[b, 0:seq_len, lanes] = a[:, :FOURIER_HEAD_DIM].astype(BF16)
            ab_scr[b, seq_len:2 * seq_len, lanes] = a[:, FOURIER_HEAD_DIM:].astype(BF16)

    f = jnp.dot(pos_scr[...], ab_scr[b], preferred_element_type=F32) * scale
    o_ref[...] = jnp.dot(f.astype(BF16), fw_ref[...], preferred_element_type=F32).astype(BF16)


def _fourier(ufour, fourier_w, layer, seq_len, name):
    m_rows = ufour.shape[0]
    w = MIXER_WIDTH
    batch = m_rows // seq_len
    tl = min(seq_len, 512)
    blocks = seq_len // tl
    pos_tab, ch_tab = _dft_tables(seq_len)
    scale = float(1.0 / np.sqrt(seq_len * FOURIER_HEAD_DIM))
    return pl.pallas_call(
        functools.partial(_fourier_kernel, seq_len, scale),
        grid=(blocks, batch),
        in_specs=[pl.BlockSpec((seq_len, w), lambda j, b: (jnp.where(j == 0, b, batch - 1), 0)),
                  pl.BlockSpec((tl, 2 * seq_len), lambda j, b: (j, 0)),
                  pl.BlockSpec((FOURIER_HEAD_DIM, 2 * FOURIER_HEAD_DIM), lambda j, b: (0, 0)),
                  pl.BlockSpec((None, w, w), lambda j, b: (layer, 0, 0))],
        out_specs=pl.BlockSpec((tl, w), lambda j, b: (b * blocks + j, 0)),
        out_shape=jax.ShapeDtypeStruct((m_rows, w), BF16),
        scratch_shapes=[pltpu.VMEM((batch, 2 * seq_len, w), BF16), pltpu.VMEM((tl, 2 * seq_len), BF16)],
        compiler_params=_params("arbitrary", "arbitrary"),
        name=name,
    )(ufour, pos_tab, ch_tab, fourier_w)


def _cast_blocks(refs):
    half = len(refs) // 2
    for src, dst in zip(refs[:half], refs[half:]):
        if len(dst.shape) == len(src.shape):
            dst[...] = src[...].astype(BF16)
        else:
            for j in range(dst.shape[0]):
                dst[j] = src[:, j * FFN_TF:(j + 1) * FFN_TF].astype(BF16)


def _ffn_weight_cast_specs(weights, layer, n_row, n_col, row_col_of):
    br, bc = D_MODEL // n_row, D_FF // n_col
    slabs = bc // FFN_TF

    def up_map(*ids):
        r, c = row_col_of(*ids)
        return (layer, r, c)

    def down_map(*ids):
        r, c = row_col_of(*ids)
        return (layer, c, r)

    def tiled_map(*ids):
        r, c = row_col_of(*ids)
        return (0, c, r, 0)

    in_specs = [pl.BlockSpec((None, br, bc), up_map), pl.BlockSpec((None, br, bc), up_map),
                pl.BlockSpec((None, bc, br), down_map)]
    out_specs = [pl.BlockSpec((None, slabs, br, FFN_TF), tiled_map),
                 pl.BlockSpec((None, slabs, br, FFN_TF), tiled_map),
                 pl.BlockSpec((None, bc, br), lambda *ids: (0,) + down_map(*ids)[1:])]
    tiled = jax.ShapeDtypeStruct((1, D_FF // FFN_TF, D_MODEL, FFN_TF), BF16)
    out_shape = [tiled, tiled, jax.ShapeDtypeStruct((1, D_FF, D_MODEL), BF16)]
    return in_specs, out_specs, out_shape


def _attn_kernel(lam_init, cached, heads, n_cast, *refs):
    n_in = (7 if cached else 5) + n_cast
    cast_src, cast_dst = refs[n_in - n_cast:n_in], refs[n_in + 1:n_in + 1 + n_cast]
    o_ref = refs[n_in]
    scratch = refs[n_in + 1 + n_cast:]
    if cached:
        q_ref, k_ref, v_ref, ck_ref, cv_ref, lam_ref, gs_ref = refs[:7]
        vt_scr, s_scr, ckb_scr = scratch
    else:
        q_ref, k_ref, v_ref, lam_ref, gs_ref = refs[:5]
        vt_scr, s_scr = scratch
    _cast_blocks(cast_src + cast_dst)
    tq = q_ref.shape[0]
    own = k_ref.shape[0]
    past = ck_ref.shape[0] if cached else 0

    @pl.when(pl.program_id(2) == 0)
    def _():
        for h in range(heads):
            cols = slice(h * V_DIM, (h + 1) * V_DIM)
            base = h * VT_ROWS
            vt_scr[base:base + V_DIM, 0:own] = v_ref[:, cols].astype(F32).T.astype(BF16)
            vt_scr[base + V_DIM:base + VT_ROWS, :] = jnp.ones((VT_ROWS - V_DIM, own + past), BF16)
            if cached:
                vt_scr[base:base + V_DIM, own:] = cv_ref[:, cols].T.astype(BF16)
                ckb_scr[:, cols] = ck_ref[:, cols].astype(BF16)

    lv = lam_ref[...]
    lam = (jnp.exp(jnp.sum(lv[0:1] * lv[1:2], axis=-1, keepdims=True))
           - jnp.exp(jnp.sum(lv[2:3] * lv[3:4], axis=-1, keepdims=True)) + lam_init)
    lane = lax.broadcasted_iota(jnp.int32, (1, V_DIM), 1)
    nt = (((1,), (1,)), ((), ()))
    segments = [(k_ref, 0, own)] + ([(ckb_scr, own, past)] if cached else [])
    items = [(h, key_ref, offset, c0, min(KEY_CHUNK, count - c0))
             for h in range(heads) for key_ref, offset, count in segments
             for c0 in range(0, count, KEY_CHUNK)]
    qqs = []
    for h in range(heads):
        q = q_ref[:, h * V_DIM:(h + 1) * V_DIM]
        zero = jnp.zeros_like(q)
        qqs.append(jnp.concatenate([jnp.where(lane < QK_DIM, q, zero),
                                    jnp.where(lane >= QK_DIM, q, zero)], axis=0))

    def scores(n):
        h, key_ref, _, c0, kc = items[n]
        s_scr[n % 2, 0:kc, :] = lax.dot_general(key_ref[c0:c0 + kc, h * V_DIM:(h + 1) * V_DIM], qqs[h], nt,
                                                preferred_element_type=F32)

    scores(0)
    parts = [[] for _ in range(heads)]
    for n, (h, _, offset, c0, kc) in enumerate(items):
        if n + 1 < len(items):
            scores(n + 1)
        s = s_scr[n % 2, 0:kc, :]
        mc = s.max(axis=0, keepdims=True)
        e = jnp.exp2(s - mc).astype(BF16)
        oc = jnp.dot(vt_scr[h * VT_ROWS:(h + 1) * VT_ROWS, offset + c0:offset + c0 + kc], e,
                     preferred_element_type=F32)
        parts[h].append((mc, oc))
    for h in range(heads):
        m = functools.reduce(jnp.maximum, [mc for mc, _ in parts[h]])
        acc = functools.reduce(jnp.add, [oc * jnp.exp2(mc - m) for mc, oc in parts[h]])
        inv = 1.0 / acc[V_DIM:V_DIM + 1]
        o_t = acc[:V_DIM, :tq] * inv[:, :tq] - lam * (acc[:V_DIM, tq:] * inv[:, tq:])
        o_ref[:, h * V_DIM:(h + 1) * V_DIM] = (_rms(o_t.T, gs_ref[...]) * (1.0 - lam_init)).astype(BF16)


def _attention(q, k, v, cache_k, cache_v, lam, g_subln, layer, lam_init, seq_len, tq, heads, cast, name):
    m_rows = q.shape[0]
    batch = m_rows // seq_len
    qb = seq_len // tq
    cached = cache_k is not None
    hw = heads * V_DIM
    in_specs = [pl.BlockSpec((tq, hw), lambda b, h, i: (b * qb + i, h)),
                pl.BlockSpec((seq_len, hw), lambda b, h, i: (b, h)),
                pl.BlockSpec((seq_len, hw), lambda b, h, i: (b, h))]
    args = [q, k, v]
    past = 0
    scratch = []
    if cached:
        past = cache_k.shape[2]
        spec = lambda: pl.BlockSpec((None, None, past, hw), lambda b, h, i: (b, layer, 0, h))
        in_specs += [spec(), spec()]
        args += [cache_k, cache_v]
        scratch = [pltpu.VMEM((past, hw), BF16)]
    in_specs += [pl.BlockSpec((None, 4, QK_DIM), lambda b, h, i: (layer, 0, 0)),
                 pl.BlockSpec((None, 1, V_DIM), lambda b, h, i: (layer, 0, 0))]
    args += [lam, g_subln.reshape(DEPTH, 1, V_DIM)]
    groups = N_HEADS // heads
    out_specs = [pl.BlockSpec((tq, hw), lambda b, h, i: (b * qb + i, h))]
    out_shape = [jax.ShapeDtypeStruct((m_rows, N_HEADS * V_DIM), BF16)]
    n_cast = 0
    if cast is not None:
        weights, cast_layer = cast
        assert groups == 1
        ci, co, cs = _ffn_weight_cast_specs(weights, cast_layer, batch * qb, 1,
                                            lambda b, h, i: (b * qb + i, 0))
        in_specs += ci
        out_specs += co
        out_shape += cs
        args += list(weights)
        n_cast = len(weights)
    outs = pl.pallas_call(
        functools.partial(_attn_kernel, lam_init, cached, heads, n_cast),
        grid=(batch, groups, qb),
        in_specs=in_specs,
        out_specs=out_specs,
        out_shape=out_shape,
        scratch_shapes=[pltpu.VMEM((heads * VT_ROWS, seq_len + past), BF16),
                        pltpu.VMEM((2, min(KEY_CHUNK, seq_len), 2 * tq), F32)] + scratch,
        compiler_params=_params("arbitrary", "arbitrary", "arbitrary"),
        name=name,
    )(*args)
    return outs[0], tuple(outs[1:])


def _ffn_kernel(n_cast, h_ref, x_ref, wg_ref, wu_ref, wd_ref, g2_ref, *rest):
    cast_src, o_ref, cast_dst, acc_scr = rest[:n_cast], rest[n_cast], rest[n_cast + 1:-1], rest[-1]
    f = pl.program_id(1)
    last = pl.num_programs(1) - 1
    _cast_blocks(cast_src + cast_dst)

    def partial_down():
        h = h_ref[...]
        g = jnp.dot(h, wg_ref[...], preferred_element_type=F32)
        u = jnp.dot(h, wu_ref[...], preferred_element_type=F32)
        a = (g * jax.nn.sigmoid(g) * u).astype(BF16)
        return jnp.dot(a, wd_ref[...], preferred_element_type=F32)

    @pl.when(f == 0)
    def _():
        acc_scr[...] = partial_down()

    @pl.when((f > 0) & (f < last))
    def _():
        acc_scr[...] += partial_down()

    @pl.when(f == last)
    def _():
        o_ref[...] = x_ref[...] + g2_ref[...] * (acc_scr[...] + partial_down())


def _ffn(h2, x1, ffn_w, wl, mod, layer, row_of, tm, cast, name):
    tf = FFN_TF
    m_rows = x1.shape[0]
    grid = (m_rows // tm, D_FF // tf)
    in_specs = [pl.BlockSpec((tm, D_MODEL), lambda i, f: (i, 0)),
                pl.BlockSpec((tm, D_MODEL), lambda i, f: (i, 0)),
                pl.BlockSpec((None, None, D_MODEL, tf), lambda i, f: (wl, f, 0, 0)),
                pl.BlockSpec((None, None, D_MODEL, tf), lambda i, f: (wl, f, 0, 0)),
                pl.BlockSpec((None, tf, D_MODEL), lambda i, f: (wl, f, 0)),
                _mod_spec(layer, 5, row_of)]
    out_specs = [pl.BlockSpec((tm, D_MODEL), lambda i, f: (i, 0))]
    out_shape = [jax.ShapeDtypeStruct((m_rows, D_MODEL), F32)]
    args = [h2, x1, *ffn_w, mod]
    n_cast = 0
    if cast is not None:
        weights, proj_weights, cast_layer = cast
        ci, co, cs = _ffn_weight_cast_specs(weights, cast_layer, grid[0], grid[1], lambda i, f: (i, f))
        for a in proj_weights:
            rows, cols = a.shape[1:]
            ci.append(pl.BlockSpec((None, rows // grid[0], cols), lambda i, f: (cast_layer, i, 0)))
            co.append(pl.BlockSpec((None, rows // grid[0], cols), lambda i, f: (0, i, 0)))
            cs.append(jax.ShapeDtypeStruct((1, rows, cols), BF16))
        in_specs += ci
        out_specs += co
        out_shape += cs
        args += list(weights) + list(proj_weights)
        n_cast = len(weights) + len(proj_weights)
    outs = pl.pallas_call(
        functools.partial(_ffn_kernel, n_cast),
        grid=grid,
        in_specs=in_specs,
        out_specs=out_specs,
        out_shape=out_shape,
        scratch_shapes=[pltpu.VMEM((tm, D_MODEL), F32)],
        compiler_params=_params("arbitrary", "arbitrary"),
        name=name,
    )(*args)
    return outs[0], tuple(outs[1:])


def kernel(x_prompt, x_sample, cache_k, cache_v, c, c_ctx, w_ada, b_ada, g_norm1, w_in, pool_w, pool_scale, g_q, g_k, lam, g_subln, conv_dw, conv_dw_b, conv_ln_g, conv_ln_b, conv_pw, conv_pw_b, fourier_w, w_out, g_norm2, w_gate, w_up, w_down):
    batch, seq, _ = x_prompt.shape
    dec_batch, dec_seq, _ = x_sample.shape
    past = cache_k.shape[2]
    tm = 512

    cc = jnp.concatenate([c_ctx[None, :], c, jnp.zeros((8 - 1 - dec_batch, D_MODEL), F32)], axis=0)
    mod = _modulation(cc, w_ada, b_ada).reshape(DEPTH, 8, 1, 6 * D_MODEL)

    bf = lambda a: a.astype(BF16)
    fourier_w_b = bf(fourier_w)
    proj_b = (bf(w_in[0:1]), bf(w_out[0:1]))
    proj_next = None
    mix_params = dict(pool_w=bf(pool_w), pool_scale=pool_scale, conv_dw=conv_dw, conv_dw_b=conv_dw_b,
                      conv_ln_g=conv_ln_g, conv_ln_b=conv_ln_b, conv_pw=bf(conv_pw), conv_pw_b=conv_pw_b)
    ffn_f32 = (w_gate, w_up, w_down)
    ck = cache_k.reshape(dec_batch, DEPTH, past, N_HEADS * V_DIM)
    cv = cache_v.reshape(dec_batch, DEPTH, past, N_HEADS * V_DIM)

    streams = {
        "lat": dict(x=x_sample.reshape(dec_batch * dec_seq, D_MODEL), seq=dec_seq, latent=True,
                    row_of=lambda i: 1 + i // (dec_seq // tm)),
        "ctx": dict(x=x_prompt.reshape(batch * seq, D_MODEL), seq=seq, latent=False,
                    row_of=lambda i: 0),
    }
    new_k, new_v = [], []
    ffn_b = None
    ffn_next = None
    for layer in range(DEPTH):
        lam_init = 0.8 - 0.6 * float(np.exp(-0.3 * layer))
        if layer > 0:
            ffn_b, proj_b = ffn_next, proj_next
        for name, st in streams.items():
            latent, seq_len = st["latent"], st["seq"]
            outs = _inproj(st["x"], mod, g_norm1, proj_b[0], g_q, g_k, mix_params, layer, seq_len, latent, tm)
            y_pool, q, k, v, y_conv, ufour = outs[:6]
            if not latent:
                new_k.append(outs[6].reshape(batch, seq, N_HEADS, 2, QK_DIM))
                new_v.append(outs[7].reshape(batch, seq, N_HEADS, V_DIM))
            y_four = _fourier(ufour, fourier_w_b, layer, seq_len, "fourier_" + name)
            att, cast_out = _attention(q, k, v, ck if latent else None, cv if latent else None, lam, g_subln,
                                       layer, lam_init, seq_len, 512 if latent else seq_len, N_HEADS,
                                       (ffn_f32, 0) if (latent and layer == 0) else None, "attn_" + name)
            if cast_out:
                ffn_b = cast_out
            x1, h2 = _outproj(st["x"], (y_pool, att, y_conv, y_four), proj_b[1], mod, g_norm2, layer,
                              st["row_of"], tm, "outproj_" + name)
            cast = (ffn_f32, (w_in, w_out), layer + 1) if (latent and layer + 1 < DEPTH) else None
            st["x"], cast_out = _ffn(h2, x1, ffn_b, 0, mod, layer, st["row_of"], tm, cast, "ffn_" + name)
            if cast_out:
                ffn_next, proj_next = cast_out[:3], cast_out[3:]
    y_prompt = streams["ctx"]["x"].reshape(batch, seq, D_MODEL)
    y_sample = streams["lat"]["x"].reshape(dec_batch, dec_seq, D_MODEL)
    return (y_prompt, y_sample, jnp.stack(new_k, axis=1), jnp.stack(new_v, axis=1))
```

```python
import functools

import numpy as np
import jax
import jax.numpy as jnp
from jax import lax
from jax.experimental import pallas as pl
from jax.experimental.pallas import tpu as pltpu

D_MODEL = 2048
DEPTH = 2
GRID_W = 64
MIXER_WIDTH = 512
POOL_GROUP_DIM = 128
POOL_WINDOWS = (2, 4, 8, 16)
N_HEADS = 4
V_DIM = 128
QK_DIM = 64
ROPE_AXIS_DIM = 32
ROPE_BASE = 10000.0
CONV_TAPS = 31
FOURIER_HEAD_DIM = 128
IN_COLS = 7 * MIXER_WIDTH
D_FF = 5632
EPS = 1e-6

LANES = 128
SUBLANES = 8
HALO = 16
SEQ_CHUNK = 256
CONV_SUB = 32
KEY_CHUNK = 512
FFN_TF = 512
VT_ROWS = V_DIM + 2 * SUBLANES
LOG2E = 1.4426950408889634
VMEM_LIMIT = 56 * 1024 * 1024

BF16 = jnp.bfloat16
F32 = jnp.float32


def _params(*semantics):
    return pltpu.CompilerParams(dimension_semantics=semantics, vmem_limit_bytes=VMEM_LIMIT)


@functools.lru_cache(maxsize=None)
def _rope_tables(seq_len):
    t = np.arange(seq_len)
    pos = np.stack([(t // GRID_W), (t % GRID_W)], axis=0).astype(np.float64)
    half = ROPE_AXIS_DIM // 2
    inv = 1.0 / (ROPE_BASE ** (np.arange(0, ROPE_AXIS_DIM, 2, dtype=np.float64) / ROPE_AXIS_DIM))
    lane = np.arange(LANES)
    d = lane % QK_DIM
    axis = d // ROPE_AXIS_DIM
    freq = d % half
    ang = pos[axis, :].T * inv[freq][None, :]
    first = ((d % ROPE_AXIS_DIM) < half)[None, :]
    cos = np.cos(ang)
    sin = np.sin(ang)
    sin_next = np.where(first, -sin, 0.0)
    sin_prev = np.where(first, 0.0, sin)
    return (cos.astype(np.float32), sin_next.astype(np.float32), sin_prev.astype(np.float32))


@functools.lru_cache(maxsize=None)
def _dft_tables(seq_len):
    def cs(n):
        k = np.arange(n)
        m = (k[:, None] * k[None, :]) % n
        a = 2.0 * np.pi * m.astype(np.float64) / n
        return np.cos(a), np.sin(a)
    cl, sl = cs(seq_len)
    cc, sc = cs(FOURIER_HEAD_DIM)
    pos = np.concatenate([cl, -sl], axis=1).astype(np.float32)
    ch = np.concatenate([cc, sc], axis=1).astype(np.float32)
    return pos, ch


@functools.lru_cache(maxsize=None)
def _group_mean_matrix():
    g = np.arange(MIXER_WIDTH // 2) // QK_DIM
    p = (g[:, None] == g[None, :]).astype(np.float32) / QK_DIM
    return jnp.asarray(p, dtype=BF16)


def _mod_kernel(c_ref, w_ref, b_ref, o_ref):
    c = c_ref[...]
    s = (c * jax.nn.sigmoid(c)).astype(BF16)
    o_ref[...] = jnp.dot(s, w_ref[...].astype(BF16), preferred_element_type=F32) + b_ref[...]


def _modulation(cc, w_ada, b_ada):
    tn = 1024
    n_cols = 6 * D_MODEL
    return pl.pallas_call(
        _mod_kernel,
        grid=(DEPTH, n_cols // tn),
        in_specs=[
            pl.BlockSpec((8, D_MODEL), lambda l, j: (0, 0)),
            pl.BlockSpec((None, D_MODEL, tn), lambda l, j: (l, 0, j)),
            pl.BlockSpec((None, 1, tn), lambda l, j: (l, 0, j)),
        ],
        out_specs=pl.BlockSpec((None, 8, tn), lambda l, j: (l, 0, j)),
        out_shape=jax.ShapeDtypeStruct((DEPTH, 8, n_cols), F32),
        compiler_params=_params("arbitrary", "arbitrary"),
        name="adaln_mod",
    )(cc, w_ada, b_ada.reshape(DEPTH, 1, n_cols))


def _mod_spec(layer, chunk, row_of_block):
    return pl.BlockSpec((None, None, 1, D_MODEL),
                        lambda i, *_: (layer, row_of_block(i), 0, chunk))


def _rms(x, gain):
    return x * lax.rsqrt(jnp.mean(x * x, axis=-1, keepdims=True) + EPS) * gain


def _rope(t, cos, sin_next, sin_prev):
    outs = []
    for c in range(MIXER_WIDTH // LANES):
        tc = t[:, c * LANES:(c + 1) * LANES]
        outs.append(tc * cos
                    + pltpu.roll(tc, LANES - ROPE_AXIS_DIM // 2, axis=1) * sin_next
                    + pltpu.roll(tc, ROPE_AXIS_DIM // 2, axis=1) * sin_prev)
    return jnp.concatenate(outs, axis=1)


def _inproj_kernel(latent, chunks_per_seq, stacked, *refs):
    xp_ref, x_ref, xn_ref, sh_ref, sc_ref, g1_ref, w_ref, gq_ref, gk_ref, p_ref = refs[:10]
    n_in = 10
    if latent:
        cos_ref, sn_ref, sp_ref = refs[n_in:n_in + 3]
        n_in += 3
    mix_refs = refs[n_in:n_in + 8]
    n_in += 8
    if stacked:
        kprev_ref, vprev_ref = refs[n_in:n_in + 2]
        n_in += 2

    def cache_out(ref, prev_ref, new):
        if stacked:
            seqs, _, seq, width = ref.shape
            ref[:, 0] = prev_ref[...].reshape(seqs, seq, width)
            ref[:, 1] = new.reshape(seqs, seq, width)
        else:
            ref[...] = new
    ypool_ref, q_ref, k_ref, v_ref, yconv_ref, ufour_ref = refs[n_in:n_in + 6]
    n_out = 6
    if not latent:
        kf_ref, vf_ref = refs[n_in + 6:n_in + 8]
        n_out = 8
    h_scr, up_scr, glu_scr, pool_scr, conv_scr, shift_scr, acc_scr = refs[n_in + n_out:]
    w = MIXER_WIDTH
    tm = x_ref.shape[0]
    own = slice(HALO, HALO + tm)

    def mod_norm(x):
        return (_rms(x, g1_ref[...]) * (1.0 + sc_ref[...]) + sh_ref[...]).astype(BF16)

    h_scr[0:HALO, :] = mod_norm(xp_ref[...])
    h_scr[own, :] = mod_norm(x_ref[...])
    h_scr[HALO + tm:, :] = mod_norm(xn_ref[...])

    def proj(n, rows=own):
        return jnp.dot(h_scr[rows, :], w_ref[:, n * w:(n + 1) * w], preferred_element_type=F32)

    def qk_norm(z, gain_ref):
        zz = (z * z).astype(BF16)
        half = p_ref.shape[0]
        ms = jnp.concatenate([jnp.dot(zz[:, :half], p_ref[...], preferred_element_type=F32),
                              jnp.dot(zz[:, half:], p_ref[...], preferred_element_type=F32)], axis=1)
        return z * lax.rsqrt(ms + EPS) * gain_ref[...]

    everything = slice(0, tm + 2 * HALO)
    up_scr[...] = proj(0, everything)
    glu_scr[...] = proj(4, everything) * jax.nn.sigmoid(proj(5, everything))

    qn = qk_norm(proj(1), gq_ref)
    if latent:
        qn = _rope(qn, cos_ref[...], sn_ref[...], sp_ref[...])
    q_ref[...] = (qn * (QK_DIM ** -0.5 * LOG2E)).astype(BF16)

    kn = qk_norm(proj(2), gk_ref)
    if latent:
        kn = _rope(kn, cos_ref[...], sn_ref[...], sp_ref[...])
    else:
        cache_out(kf_ref, kprev_ref if stacked else None, kn)
    k_ref[...] = kn.astype(BF16)

    v = proj(3)
    if not latent:
        cache_out(vf_ref, vprev_ref if stacked else None, v)
    v_ref[...] = v.astype(BF16)
    ufour_ref[...] = proj(6).astype(BF16)

    chunks = tm // SEQ_CHUNK
    for c in range(chunks):
        lo = c * SEQ_CHUNK
        views = [scr.at[rows, :] for scr in (up_scr, glu_scr)
                 for rows in (slice(lo, lo + HALO), slice(lo + HALO, lo + HALO + SEQ_CHUNK),
                              slice(lo + HALO + SEQ_CHUNK, lo + 2 * HALO + SEQ_CHUNK))]
        y_pool, y_conv = _seqmix_chunk(pl.program_id(0) * chunks + c, chunks_per_seq, *views, *mix_refs,
                                       pool_scr, conv_scr, shift_scr, acc_scr)
        ypool_ref[lo:lo + SEQ_CHUNK, :] = y_pool
        yconv_ref[lo:lo + SEQ_CHUNK, :] = y_conv


def _inproj(x, mod, g_norm1, w_in, g_q, g_k, p, layer, seq_len, latent, tm, prev_kv=None):
    m_rows = x.shape[0]
    w = MIXER_WIDTH
    if latent:
        row_of = lambda i: 1 + i // (seq_len // tm)
    else:
        row_of = lambda i: 0
    hb = tm // HALO
    n_halo_blocks = m_rows // HALO
    row_spec = lambda: pl.BlockSpec((tm, w), lambda i: (i, 0))
    vec_spec = lambda: pl.BlockSpec((1, w), lambda i: (0, 0))
    layer_vec = lambda a: a.reshape(DEPTH, 1, a.shape[-1])
    layer_vec_spec = lambda n=w: pl.BlockSpec((None, 1, n), lambda i: (layer, 0, 0))
    in_specs = [
        pl.BlockSpec((HALO, D_MODEL), lambda i: (jnp.maximum(i * hb - 1, 0), 0)),
        pl.BlockSpec((tm, D_MODEL), lambda i: (i, 0)),
        pl.BlockSpec((HALO, D_MODEL), lambda i: (jnp.minimum((i + 1) * hb, n_halo_blocks - 1), 0)),
        _mod_spec(layer, 0, row_of),
        _mod_spec(layer, 1, row_of),
        layer_vec_spec(D_MODEL),
        pl.BlockSpec((None, D_MODEL, IN_COLS), lambda i: (layer, 0, 0), pipeline_mode=pl.Buffered(1)),
        vec_spec(), vec_spec(),
        pl.BlockSpec((w // 2, w // 2), lambda i: (0, 0)),
    ]
    args = [x, x, x, mod, mod, layer_vec(g_norm1), w_in,
            jnp.tile(g_q[layer], w // QK_DIM).reshape(1, w),
            jnp.tile(g_k[layer], w // QK_DIM).reshape(1, w),
            _group_mean_matrix()]
    if latent:
        blocks_per_seq = seq_len // tm
        for tab in _rope_tables(seq_len):
            in_specs.append(pl.BlockSpec((tm, LANES), lambda i: (i % blocks_per_seq, 0)))
            args.append(jnp.asarray(tab))
    in_specs += [pl.BlockSpec((None, len(POOL_WINDOWS), POOL_GROUP_DIM, POOL_GROUP_DIM),
                              lambda i: (layer, 0, 0, 0)),
                 layer_vec_spec(),
                 pl.BlockSpec((None, CONV_TAPS, SUBLANES, w), lambda i: (layer, 0, 0, 0)),
                 layer_vec_spec(), layer_vec_spec(), layer_vec_spec(),
                 pl.BlockSpec((None, w, w), lambda i: (layer, 0, 0)),
                 layer_vec_spec()]
    args += [p["pool_w"], layer_vec(p["pool_scale"]),
             jnp.broadcast_to(p["conv_dw"][:, :, None, :], (DEPTH, CONV_TAPS, SUBLANES, w)),
             layer_vec(p["conv_dw_b"]), layer_vec(p["conv_ln_g"]), layer_vec(p["conv_ln_b"]),
             p["conv_pw"], layer_vec(p["conv_pw_b"])]
    out_shape = [jax.ShapeDtypeStruct((m_rows, w), BF16)] * 6
    out_specs = [row_spec() for _ in range(6)]
    stacked = prev_kv is not None
    if stacked:
        assert DEPTH == 2 and layer == 1 and not latent
        seqs = tm // seq_len
        in_specs += [row_spec(), row_spec()]
        args += list(prev_kv)
        out_shape += [jax.ShapeDtypeStruct((m_rows // seq_len, DEPTH, seq_len, w), F32)] * 2
        out_specs += [pl.BlockSpec((seqs, DEPTH, seq_len, w), lambda i: (i, 0, 0, 0)) for _ in range(2)]
    elif not latent:
        out_shape += [jax.ShapeDtypeStruct((m_rows, w), F32)] * 2
        out_specs += [row_spec(), row_spec()]
    r = SEQ_CHUNK
    return pl.pallas_call(
        functools.partial(_inproj_kernel, latent, seq_len // r, stacked),
        grid=(m_rows // tm,),
        in_specs=in_specs,
        out_specs=out_specs,
        out_shape=out_shape,
        scratch_shapes=[pltpu.VMEM((tm + 2 * HALO, D_MODEL), BF16),
                        pltpu.VMEM((tm + 2 * HALO, w), F32), pltpu.VMEM((tm + 2 * HALO, w), F32),
                        pltpu.VMEM((r + 2 * HALO, w), F32), pltpu.VMEM((r + 2 * HALO, w), F32),
                        pltpu.VMEM((SUBLANES - 1, r + SUBLANES * (CONV_TAPS // SUBLANES), w), F32),
                        pltpu.VMEM((r, w), F32)],
        compiler_params=_params("arbitrary"),
        name="inproj_lat" if latent else "inproj_ctx",
    )(*args)


def _shift_rows(x, s):
    return pltpu.roll(x, s % x.shape[0], axis=0)


def _seqmix_chunk(i, chunks_per_seq, up_prev, up_cur, up_next, gl_prev, gl_cur, gl_next,
                  pw_ref, ps_ref, dw_ref, dwb_ref, lng_ref, lnb_ref, cw_ref, cwb_ref,
                  pool_scr, conv_scr, shift_scr, acc_scr):
    r = SEQ_CHUNK
    first = (i % chunks_per_seq) == 0
    last = (i % chunks_per_seq) == chunks_per_seq - 1
    zero_halo = jnp.zeros((HALO, MIXER_WIDTH), F32)

    def fill(scr, prev, cur, nxt):
        scr[0:HALO, :] = jnp.where(first, zero_halo, prev[...])
        scr[HALO:HALO + r, :] = cur[...]
        scr[HALO + r:2 * HALO + r, :] = jnp.where(last, zero_halo, nxt[...])

    fill(pool_scr, up_prev, up_cur, up_next)
    fill(conv_scr, gl_prev, gl_cur, gl_next)

    row = lax.broadcasted_iota(jnp.int32, (r, 1), 0) + (i % chunks_per_seq) * r
    seq_len = chunks_per_seq * r
    outs = []
    for g, win in enumerate(POOL_WINDOWS):
        lanes = slice(g * POOL_GROUP_DIM, (g + 1) * POOL_GROUP_DIM)
        u = pool_scr[:, lanes]
        s = _shift_rows(u, 1) + u
        span = 2
        while span < win:
            s = _shift_rows(s, span // 2) + _shift_rows(s, -(span // 2))
            span *= 2
        lo = jnp.maximum(row - win // 2, 0)
        hi = jnp.minimum(row + win // 2 - 1, seq_len - 1)
        cnt = (hi - lo + 1).astype(F32)
        p = s[HALO:HALO + r] / cnt - u[HALO:HALO + r]
        outs.append(jnp.dot(p.astype(BF16), pw_ref[g], preferred_element_type=F32))
    y_pool = (jnp.concatenate(outs, axis=1) * ps_ref[...]).astype(BF16)

    shifted_rows = r + SUBLANES * (CONV_TAPS // SUBLANES)
    for lo in range(1, SUBLANES):
        shift_scr[lo - 1] = conv_scr[lo:lo + shifted_rows, :]
    groups = CONV_SUB // SUBLANES
    for c in range(r // CONV_SUB):
        acc = jnp.zeros((groups, SUBLANES, MIXER_WIDTH), F32)
        for j in range(CONV_TAPS):
            hi, lo = divmod(j + 1, SUBLANES)
            src = conv_scr if lo == 0 else shift_scr.at[lo - 1]
            start = c * CONV_SUB + SUBLANES * hi
            rows = src[start:start + CONV_SUB, :].reshape(groups, SUBLANES, MIXER_WIDTH)
            acc = acc + rows * dw_ref[j]
        acc_scr[c * CONV_SUB:(c + 1) * CONV_SUB, :] = acc.reshape(CONV_SUB, MIXER_WIDTH)
    y = acc_scr[...] + dwb_ref[...]
    mu = jnp.mean(y, axis=-1, keepdims=True)
    yc = y - mu
    yn = yc * lax.rsqrt(jnp.mean(yc * yc, axis=-1, keepdims=True) + EPS) * lng_ref[...] + lnb_ref[...]
    act = yn * jax.nn.sigmoid(yn)
    y_conv = (jnp.dot(act.astype(BF16), cw_ref[...], preferred_element_type=F32)
              + cwb_ref[...]).astype(BF16)
    return y_pool, y_conv


def _outproj_kernel(x_ref, p0, p1, p2, p3, w_ref, g1_ref, sh_ref, sc_ref, gn_ref, x1_ref, h2_ref):
    half = x_ref.shape[0] // 2
    for rows in (slice(0, half), slice(half, 2 * half)):
        mix = jnp.concatenate([p0[rows, :], p1[rows, :], p2[rows, :], p3[rows, :]], axis=1)
        a = jnp.dot(mix, w_ref[...], preferred_element_type=F32)
        x1 = x_ref[rows, :] + g1_ref[...] * a
        x1_ref[rows, :] = x1
        h2_ref[rows, :] = (_rms(x1, gn_ref[...]) * (1.0 + sc_ref[...]) + sh_ref[...]).astype(BF16)


def _outproj(x, parts, w_out, mod, g_norm2, layer, row_of, tm, name):
    m_rows = x.shape[0]
    w = MIXER_WIDTH
    row_spec = lambda: pl.BlockSpec((tm, D_MODEL), lambda i: (i, 0))
    part_spec = lambda: pl.BlockSpec((tm, w), lambda i: (i, 0))
    return pl.pallas_call(
        _outproj_kernel,
        grid=(m_rows // tm,),
        in_specs=[row_spec(), part_spec(), part_spec(), part_spec(), part_spec(),
                  pl.BlockSpec((None, D_MODEL, D_MODEL), lambda i: (layer, 0, 0), pipeline_mode=pl.Buffered(1)),
                  _mod_spec(layer, 2, row_of), _mod_spec(layer, 3, row_of), _mod_spec(layer, 4, row_of),
                  pl.BlockSpec((None, 1, D_MODEL), lambda i: (layer, 0, 0))],
        out_specs=[row_spec(), row_spec()],
        out_shape=[jax.ShapeDtypeStruct((m_rows, D_MODEL), F32),
                   jax.ShapeDtypeStruct((m_rows, D_MODEL), BF16)],
        compiler_params=_params("arbitrary"),
        name=name,
    )(x, *parts, w_out, mod, mod, mod, g_norm2.reshape(DEPTH, 1, D_MODEL))


def _fourier_kernel(seq_len, scale, u_ref, pos_ref, ch_ref, fw_ref, o_ref, ab_scr, pos_scr):
    b = pl.program_id(1)

    @pl.when(b == 0)
    def _():
        pos_scr[...] = pos_ref[...].astype(BF16)

    @pl.when(pl.program_id(0) == 0)
    def _():
        ch = ch_ref[...].astype(BF16)
        for h in range(MIXER_WIDTH // FOURIER_HEAD_DIM):
            lanes = slice(h * FOURIER_HEAD_DIM, (h + 1) * FOURIER_HEAD_DIM)
            a = jnp.dot(u_ref[:, lanes], ch, preferred_element_type=F32)
            ab_scr[b, 0:seq_len, lanes] = a[:, :FOURIER_HEAD_DIM].astype(BF16)
            ab_scr[b, seq_len:2 * seq_len, lanes] = a[:, FOURIER_HEAD_DIM:].astype(BF16)

    f = jnp.dot(pos_scr[...], ab_scr[b], preferred_element_type=F32) * scale
    o_ref[...] = jnp.dot(f.astype(BF16), fw_ref[...], preferred_element_type=F32).astype(BF16)


def _fourier(ufour, fourier_w, layer, seq_len, name):
    m_rows = ufour.shape[0]
    w = MIXER_WIDTH
    batch = m_rows // seq_len
    tl = min(seq_len, 512)
    blocks = seq_len // tl
    pos_tab, ch_tab = _dft_tables(seq_len)
    scale = float(1.0 / np.sqrt(seq_len * FOURIER_HEAD_DIM))
    return pl.pallas_call(
        functools.partial(_fourier_kernel, seq_len, scale),
        grid=(blocks, batch),
        in_specs=[pl.BlockSpec((seq_len, w), lambda j, b: (jnp.where(j == 0, b, batch - 1), 0)),
                  pl.BlockSpec((tl, 2 * seq_len), lambda j, b: (j, 0)),
                  pl.BlockSpec((FOURIER_HEAD_DIM, 2 * FOURIER_HEAD_DIM), lambda j, b: (0, 0)),
                  pl.BlockSpec((None, w, w), lambda j, b: (layer, 0, 0))],
        out_specs=pl.BlockSpec((tl, w), lambda j, b: (b * blocks + j, 0)),
        out_shape=jax.ShapeDtypeStruct((m_rows, w), BF16),
        scratch_shapes=[pltpu.VMEM((batch, 2 * seq_len, w), BF16), pltpu.VMEM((tl, 2 * seq_len), BF16)],
        compiler_params=_params("arbitrary", "arbitrary"),
        name=name,
    )(ufour, pos_tab, ch_tab, fourier_w)


def _cast_blocks(refs):
    half = len(refs) // 2
    for src, dst in zip(refs[:half], refs[half:]):
        if len(dst.shape) == len(src.shape):
            dst[...] = src[...].astype(BF16)
        else:
            for j in range(dst.shape[0]):
                dst[j] = src[:, j * FFN_TF:(j + 1) * FFN_TF].astype(BF16)


def _ffn_weight_cast_specs(weights, layer, n_row, n_col, row_col_of):
    br, bc = D_MODEL // n_row, D_FF // n_col
    slabs = bc // FFN_TF

    def up_map(*ids):
        r, c = row_col_of(*ids)
        return (layer, r, c)

    def down_map(*ids):
        r, c = row_col_of(*ids)
        return (layer, c, r)

    def tiled_map(*ids):
        r, c = row_col_of(*ids)
        return (0, c, r, 0)

    in_specs = [pl.BlockSpec((None, br, bc), up_map), pl.BlockSpec((None, br, bc), up_map),
                pl.BlockSpec((None, bc, br), down_map)]
    out_specs = [pl.BlockSpec((None, slabs, br, FFN_TF), tiled_map),
                 pl.BlockSpec((None, slabs, br, FFN_TF), tiled_map),
                 pl.BlockSpec((None, bc, br), lambda *ids: (0,) + down_map(*ids)[1:])]
    tiled = jax.ShapeDtypeStruct((1, D_FF // FFN_TF, D_MODEL, FFN_TF), BF16)
    out_shape = [tiled, tiled, jax.ShapeDtypeStruct((1, D_FF, D_MODEL), BF16)]
    return in_specs, out_specs, out_shape


def _attn_kernel(lam_init, cached, heads, n_cast, *refs):
    n_in = (7 if cached else 5) + n_cast
    cast_src, cast_dst = refs[n_in - n_cast:n_in], refs[n_in + 1:n_in + 1 + n_cast]
    o_ref = refs[n_in]
    scratch = refs[n_in + 1 + n_cast:]
    if cached:
        q_ref, k_ref, v_ref, ck_ref, cv_ref, lam_ref, gs_ref = refs[:7]
        vt_scr, s_scr, ckb_scr = scratch
    else:
        q_ref, k_ref, v_ref, lam_ref, gs_ref = refs[:5]
        vt_scr, s_scr = scratch
    _cast_blocks(cast_src + cast_dst)
    tq = q_ref.shape[0]
    own = k_ref.shape[0]
    past = ck_ref.shape[0] if cached else 0

    @pl.when(pl.program_id(2) == 0)
    def _():
        for h in range(heads):
            cols = slice(h * V_DIM, (h + 1) * V_DIM)
            base = h * VT_ROWS
            vt_scr[base:base + V_DIM, 0:own] = v_ref[:, cols].astype(F32).T.astype(BF16)
            vt_scr[base + V_DIM:base + VT_ROWS, :] = jnp.ones((VT_ROWS - V_DIM, own + past), BF16)
            if cached:
                vt_scr[base:base + V_DIM, own:] = cv_ref[:, cols].T.astype(BF16)
                ckb_scr[:, cols] = ck_ref[:, cols].astype(BF16)

    lv = lam_ref[...]
    lam = (jnp.exp(jnp.sum(lv[0:1] * lv[1:2], axis=-1, keepdims=True))
           - jnp.exp(jnp.sum(lv[2:3] * lv[3:4], axis=-1, keepdims=True)) + lam_init)
    lane = lax.broadcasted_iota(jnp.int32, (1, V_DIM), 1)
    nt = (((1,), (1,)), ((), ()))
    segments = [(k_ref, 0, own)] + ([(ckb_scr, own, past)] if cached else [])
    items = [(h, key_ref, offset, c0, min(KEY_CHUNK, count - c0))
             for h in range(heads) for key_ref, offset, count in segments
             for c0 in range(0, count, KEY_CHUNK)]
    qqs = []
    for h in range(heads):
        q = q_ref[:, h * V_DIM:(h + 1) * V_DIM]
        zero = jnp.zeros_like(q)
        qqs.append(jnp.concatenate([jnp.where(lane < QK_DIM, q, zero),
                                    jnp.where(lane >= QK_DIM, q, zero)], axis=0))

    def scores(n):
        h, key_ref, _, c0, kc = items[n]
        s_scr[n % 2, 0:kc, :] = lax.dot_general(key_ref[c0:c0 + kc, h * V_DIM:(h + 1) * V_DIM], qqs[h], nt,
                                                preferred_element_type=F32)

    scores(0)
    parts = [[] for _ in range(heads)]
    for n, (h, _, offset, c0, kc) in enumerate(items):
        if n + 1 < len(items):
            scores(n + 1)
        s = s_scr[n % 2, 0:kc, :]
        mc = s.max(axis=0, keepdims=True)
        e = jnp.exp2(s - mc).astype(BF16)
        oc = jnp.dot(vt_scr[h * VT_ROWS:(h + 1) * VT_ROWS, offset + c0:offset + c0 + kc], e,
                     preferred_element_type=F32)
        parts[h].append((mc, oc))
    for h in range(heads):
        m = functools.reduce(jnp.maximum, [mc for mc, _ in parts[h]])
        acc = functools.reduce(jnp.add, [oc * jnp.exp2(mc - m) for mc, oc in parts[h]])
        inv = 1.0 / acc[V_DIM:V_DIM + 1]
        o_t = acc[:V_DIM, :tq] * inv[:, :tq] - lam * (acc[:V_DIM, tq:] * inv[:, tq:])
        o_ref[:, h * V_DIM:(h + 1) * V_DIM] = (_rms(o_t.T, gs_ref[...]) * (1.0 - lam_init)).astype(BF16)


def _attention(q, k, v, cache_k, cache_v, lam, g_subln, layer, lam_init, seq_len, tq, heads, cast, name):
    m_rows = q.shape[0]
    batch = m_rows // seq_len
    qb = seq_len // tq
    cached = cache_k is not None
    hw = heads * V_DIM
    in_specs = [pl.BlockSpec((tq, hw), lambda b, h, i: (b * qb + i, h)),
                pl.BlockSpec((seq_len, hw), lambda b, h, i: (b, h)),
                pl.BlockSpec((seq_len, hw), lambda b, h, i: (b, h))]
    args = [q, k, v]
    past = 0
    scratch = []
    if cached:
        past = cache_k.shape[2]
        spec = lambda: pl.BlockSpec((None, None, past, hw), lambda b, h, i: (b, layer, 0, h))
        in_specs += [spec(), spec()]
        args += [cache_k, cache_v]
        scratch = [pltpu.VMEM((past, hw), BF16)]
    in_specs += [pl.BlockSpec((None, 4, QK_DIM), lambda b, h, i: (layer, 0, 0)),
                 pl.BlockSpec((None, 1, V_DIM), lambda b, h, i: (layer, 0, 0))]
    args += [lam, g_subln.reshape(DEPTH, 1, V_DIM)]
    groups = N_HEADS // heads
    out_specs = [pl.BlockSpec((tq, hw), lambda b, h, i: (b * qb + i, h))]
    out_shape = [jax.ShapeDtypeStruct((m_rows, N_HEADS * V_DIM), BF16)]
    n_cast = 0
    if cast is not None:
        weights, cast_layer = cast
        assert groups == 1
        ci, co, cs = _ffn_weight_cast_specs(weights, cast_layer, batch * qb, 1,
                                            lambda b, h, i: (b * qb + i, 0))
        in_specs += ci
        out_specs += co
        out_shape += cs
        args += list(weights)
        n_cast = len(weights)
    outs = pl.pallas_call(
        functools.partial(_attn_kernel, lam_init, cached, heads, n_cast),
        grid=(batch, groups, qb),
        in_specs=in_specs,
        out_specs=out_specs,
        out_shape=out_shape,
        scratch_shapes=[pltpu.VMEM((heads * VT_ROWS, seq_len + past), BF16),
                        pltpu.VMEM((2, min(KEY_CHUNK, seq_len), 2 * tq), F32)] + scratch,
        compiler_params=_params("arbitrary", "arbitrary", "arbitrary"),
        name=name,
    )(*args)
    return outs[0], tuple(outs[1:])


def _ffn_kernel(n_cast, h_ref, x_ref, wg_ref, wu_ref, wd_ref, g2_ref, *rest):
    cast_src, o_ref, cast_dst, acc_scr = rest[:n_cast], rest[n_cast], rest[n_cast + 1:-1], rest[-1]
    f = pl.program_id(1)
    last = pl.num_programs(1) - 1
    _cast_blocks(cast_src + cast_dst)

    def partial_down():
        h = h_ref[...]
        g = jnp.dot(h, wg_ref[...], preferred_element_type=F32)
        u = jnp.dot(h, wu_ref[...], preferred_element_type=F32)
        a = (g * jax.nn.sigmoid(g) * u).astype(BF16)
        return jnp.dot(a, wd_ref[...], preferred_element_type=F32)

    @pl.when(f == 0)
    def _():
        acc_scr[...] = partial_down()

    @pl.when((f > 0) & (f < last))
    def _():
        acc_scr[...] += partial_down()

    @pl.when(f == last)
    def _():
        o_ref[...] = x_ref[...] + g2_ref[...] * (acc_scr[...] + partial_down())


def _ffn(h2, x1, ffn_w, wl, mod, layer, row_of, tm, cast, name):
    tf = FFN_TF
    m_rows = x1.shape[0]
    grid = (m_rows // tm, D_FF // tf)
    in_specs = [pl.BlockSpec((tm, D_MODEL), lambda i, f: (i, 0)),
                pl.BlockSpec((tm, D_MODEL), lambda i, f: (i, 0)),
                pl.BlockSpec((None, None, D_MODEL, tf), lambda i, f: (wl, f, 0, 0)),
                pl.BlockSpec((None, None, D_MODEL, tf), lambda i, f: (wl, f, 0, 0)),
                pl.BlockSpec((None, tf, D_MODEL), lambda i, f: (wl, f, 0)),
                _mod_spec(layer, 5, row_of)]
    out_specs = [pl.BlockSpec((tm, D_MODEL), lambda i, f: (i, 0))]
    out_shape = [jax.ShapeDtypeStruct((m_rows, D_MODEL), F32)]
    args = [h2, x1, *ffn_w, mod]
    n_cast = 0
    if cast is not None:
        weights, cast_layer = cast
        ci, co, cs = _ffn_weight_cast_specs(weights, cast_layer, grid[0], grid[1], lambda i, f: (i, f))
        in_specs += ci
        out_specs += co
        out_shape += cs
        args += list(weights)
        n_cast = len(weights)
    outs = pl.pallas_call(
        functools.partial(_ffn_kernel, n_cast),
        grid=grid,
        in_specs=in_specs,
        out_specs=out_specs,
        out_shape=out_shape,
        scratch_shapes=[pltpu.VMEM((tm, D_MODEL), F32)],
        compiler_params=_params("arbitrary", "arbitrary"),
        name=name,
    )(*args)
    return outs[0], tuple(outs[1:])


def kernel(x_prompt, x_sample, cache_k, cache_v, c, c_ctx, w_ada, b_ada, g_norm1, w_in, pool_w, pool_scale, g_q, g_k, lam, g_subln, conv_dw, conv_dw_b, conv_ln_g, conv_ln_b, conv_pw, conv_pw_b, fourier_w, w_out, g_norm2, w_gate, w_up, w_down):
    batch, seq, _ = x_prompt.shape
    dec_batch, dec_seq, _ = x_sample.shape
    past = cache_k.shape[2]
    tm = 512

    cc = jnp.concatenate([c_ctx[None, :], c, jnp.zeros((8 - 1 - dec_batch, D_MODEL), F32)], axis=0)
    mod = _modulation(cc, w_ada, b_ada).reshape(DEPTH, 8, 1, 6 * D_MODEL)

    bf = lambda a: a.astype(BF16)
    w_in_b, w_out_b, fourier_w_b = bf(w_in), bf(w_out), bf(fourier_w)
    mix_params = dict(pool_w=bf(pool_w), pool_scale=pool_scale, conv_dw=conv_dw, conv_dw_b=conv_dw_b,
                      conv_ln_g=conv_ln_g, conv_ln_b=conv_ln_b, conv_pw=bf(conv_pw), conv_pw_b=conv_pw_b)
    ffn_f32 = (w_gate, w_up, w_down)
    ck = cache_k.reshape(dec_batch, DEPTH, past, N_HEADS * V_DIM)
    cv = cache_v.reshape(dec_batch, DEPTH, past, N_HEADS * V_DIM)

    streams = {
        "lat": dict(x=x_sample.reshape(dec_batch * dec_seq, D_MODEL), seq=dec_seq, latent=True,
                    row_of=lambda i: 1 + i // (dec_seq // tm)),
        "ctx": dict(x=x_prompt.reshape(batch * seq, D_MODEL), seq=seq, latent=False,
                    row_of=lambda i: 0),
    }
    cache_kv = None
    ffn_b = None
    ffn_next = None
    for layer in range(DEPTH):
        lam_init = 0.8 - 0.6 * float(np.exp(-0.3 * layer))
        if layer > 0:
            ffn_b = ffn_next
        for name, st in streams.items():
            latent, seq_len = st["latent"], st["seq"]
            outs = _inproj(st["x"], mod, g_norm1, w_in_b, g_q, g_k, mix_params, layer, seq_len, latent, tm,
                           prev_kv=cache_kv if (not latent and layer > 0) else None)
            y_pool, q, k, v, y_conv, ufour = outs[:6]
            if not latent:
                cache_kv = (outs[6], outs[7])
            y_four = _fourier(ufour, fourier_w_b, layer, seq_len, "fourier_" + name)
            att, cast_out = _attention(q, k, v, ck if latent else None, cv if latent else None, lam, g_subln,
                                       layer, lam_init, seq_len, 512 if latent else seq_len, N_HEADS,
                                       (ffn_f32, 0) if (latent and layer == 0) else None, "attn_" + name)
            if cast_out:
                ffn_b = cast_out
            x1, h2 = _outproj(st["x"], (y_pool, att, y_conv, y_four), w_out_b, mod, g_norm2, layer,
                              st["row_of"], tm, "outproj_" + name)
            cast = (ffn_f32, layer + 1) if (latent and layer + 1 < DEPTH) else None
            st["x"], cast_out = _ffn(h2, x1, ffn_b, 0, mod, layer, st["row_of"], tm, cast, "ffn_" + name)
            if cast_out:
                ffn_next = cast_out
    y_prompt = streams["ctx"]["x"].reshape(batch, seq, D_MODEL)
    y_sample = streams["lat"]["x"].reshape(dec_batch, dec_seq, D_MODEL)
    new_k = cache_kv[0].reshape(batch, DEPTH, seq, N_HEADS, 2, QK_DIM)
    new_v = cache_kv[1].reshape(batch, DEPTH, seq, N_HEADS, V_DIM)
    return (y_prompt, y_sample, new_k, new_v)
```

```python
import functools

import numpy as np
import jax
import jax.numpy as jnp
from jax import lax
from jax.experimental import pallas as pl
from jax.experimental.pallas import tpu as pltpu

D_MODEL = 2048
DEPTH = 2
GRID_W = 64
MIXER_WIDTH = 512
POOL_GROUP_DIM = 128
POOL_WINDOWS = (2, 4, 8, 16)
N_HEADS = 4
V_DIM = 128
QK_DIM = 64
ROPE_AXIS_DIM = 32
ROPE_BASE = 10000.0
CONV_TAPS = 31
FOURIER_HEAD_DIM = 128
IN_COLS = 7 * MIXER_WIDTH
D_FF = 5632
EPS = 1e-6

LANES = 128
SUBLANES = 8
HALO = 16
SEQ_CHUNK = 256
CONV_SUB = 32
KEY_CHUNK = 512
FFN_TF = 512
VT_ROWS = V_DIM + 2 * SUBLANES
LOG2E = 1.4426950408889634
VMEM_LIMIT = 56 * 1024 * 1024

BF16 = jnp.bfloat16
F32 = jnp.float32


def _params(*semantics):
    return pltpu.CompilerParams(dimension_semantics=semantics, vmem_limit_bytes=VMEM_LIMIT)


@functools.lru_cache(maxsize=None)
def _rope_tables(seq_len):
    t = np.arange(seq_len)
    pos = np.stack([(t // GRID_W), (t % GRID_W)], axis=0).astype(np.float64)
    half = ROPE_AXIS_DIM // 2
    inv = 1.0 / (ROPE_BASE ** (np.arange(0, ROPE_AXIS_DIM, 2, dtype=np.float64) / ROPE_AXIS_DIM))
    lane = np.arange(LANES)
    d = lane % QK_DIM
    axis = d // ROPE_AXIS_DIM
    freq = d % half
    ang = pos[axis, :].T * inv[freq][None, :]
    first = ((d % ROPE_AXIS_DIM) < half)[None, :]
    cos = np.cos(ang)
    sin = np.sin(ang)
    sin_next = np.where(first, -sin, 0.0)
    sin_prev = np.where(first, 0.0, sin)
    return (cos.astype(np.float32), sin_next.astype(np.float32), sin_prev.astype(np.float32))


@functools.lru_cache(maxsize=None)
def _dft_tables(seq_len):
    def cs(n):
        k = np.arange(n)
        m = (k[:, None] * k[None, :]) % n
        a = 2.0 * np.pi * m.astype(np.float64) / n
        return np.cos(a), np.sin(a)
    cl, sl = cs(seq_len)
    cc, sc = cs(FOURIER_HEAD_DIM)
    pos = np.concatenate([cl, -sl], axis=1).astype(np.float32)
    ch = np.concatenate([cc, sc], axis=1).astype(np.float32)
    return pos, ch


@functools.lru_cache(maxsize=None)
def _group_mean_matrix():
    g = np.arange(MIXER_WIDTH // 2) // QK_DIM
    p = (g[:, None] == g[None, :]).astype(np.float32) / QK_DIM
    return jnp.asarray(p, dtype=BF16)


def _mod_kernel(c_ref, w_ref, b_ref, o_ref):
    c = c_ref[...]
    s = (c * jax.nn.sigmoid(c)).astype(BF16)
    o_ref[...] = jnp.dot(s, w_ref[...].astype(BF16), preferred_element_type=F32) + b_ref[...]


def _modulation(cc, w_ada, b_ada):
    tn = 1024
    n_cols = 6 * D_MODEL
    return pl.pallas_call(
        _mod_kernel,
        grid=(DEPTH, n_cols // tn),
        in_specs=[
            pl.BlockSpec((8, D_MODEL), lambda l, j: (0, 0)),
            pl.BlockSpec((None, D_MODEL, tn), lambda l, j: (l, 0, j)),
            pl.BlockSpec((None, 1, tn), lambda l, j: (l, 0, j)),
        ],
        out_specs=pl.BlockSpec((None, 8, tn), lambda l, j: (l, 0, j)),
        out_shape=jax.ShapeDtypeStruct((DEPTH, 8, n_cols), F32),
        compiler_params=_params("arbitrary", "arbitrary"),
        name="adaln_mod",
    )(cc, w_ada, b_ada.reshape(DEPTH, 1, n_cols))


def _mod_spec(layer, chunk, row_of_block):
    return pl.BlockSpec((None, None, 1, D_MODEL),
                        lambda i, *_: (layer, row_of_block(i), 0, chunk))


def _rms(x, gain):
    return x * lax.rsqrt(jnp.mean(x * x, axis=-1, keepdims=True) + EPS) * gain


def _rope(t, cos, sin_next, sin_prev):
    outs = []
    for c in range(MIXER_WIDTH // LANES):
        tc = t[:, c * LANES:(c + 1) * LANES]
        outs.append(tc * cos
                    + pltpu.roll(tc, LANES - ROPE_AXIS_DIM // 2, axis=1) * sin_next
                    + pltpu.roll(tc, ROPE_AXIS_DIM // 2, axis=1) * sin_prev)
    return jnp.concatenate(outs, axis=1)


def _inproj_kernel(latent, chunks_per_seq, stacked, n_cast, *refs):
    xp_ref, x_ref, xn_ref, sh_ref, sc_ref, g1_ref, w_ref, gq_ref, gk_ref, p_ref = refs[:10]
    n_in = 10
    if latent:
        cos_ref, sn_ref, sp_ref = refs[n_in:n_in + 3]
        n_in += 3
    mix_refs = refs[n_in:n_in + 8]
    n_in += 8
    if stacked:
        kprev_ref, vprev_ref = refs[n_in:n_in + 2]
        n_in += 2
    cast_src = refs[n_in:n_in + n_cast]
    n_in += n_cast

    def cache_out(ref, prev_ref, new):
        if stacked:
            seqs, _, seq, width = ref.shape
            ref[:, 0] = prev_ref[...].reshape(seqs, seq, width)
            ref[:, 1] = new.reshape(seqs, seq, width)
        else:
            ref[...] = new
    ypool_ref, q_ref, k_ref, v_ref, yconv_ref, ufour_ref = refs[n_in:n_in + 6]
    n_out = 6
    if not latent:
        kf_ref, vf_ref = refs[n_in + 6:n_in + 8]
        n_out = 8
    cast_dst = refs[n_in + n_out:n_in + n_out + n_cast]
    n_out += n_cast
    h_scr, up_scr, glu_scr, pool_scr, conv_scr, shift_scr, acc_scr = refs[n_in + n_out:]
    _cast_blocks(cast_src + cast_dst)
    w = MIXER_WIDTH
    tm = x_ref.shape[0]
    own = slice(HALO, HALO + tm)

    def mod_norm(x):
        return (_rms(x, g1_ref[...]) * (1.0 + sc_ref[...]) + sh_ref[...]).astype(BF16)

    h_scr[0:HALO, :] = mod_norm(xp_ref[...])
    h_scr[own, :] = mod_norm(x_ref[...])
    h_scr[HALO + tm:, :] = mod_norm(xn_ref[...])

    def proj(n, rows=own):
        return jnp.dot(h_scr[rows, :], w_ref[:, n * w:(n + 1) * w], preferred_element_type=F32)

    def qk_norm(z, gain_ref):
        zz = (z * z).astype(BF16)
        half = p_ref.shape[0]
        ms = jnp.concatenate([jnp.dot(zz[:, :half], p_ref[...], preferred_element_type=F32),
                              jnp.dot(zz[:, half:], p_ref[...], preferred_element_type=F32)], axis=1)
        return z * lax.rsqrt(ms + EPS) * gain_ref[...]

    everything = slice(0, tm + 2 * HALO)
    up_scr[...] = proj(0, everything)
    glu_scr[...] = proj(4, everything) * jax.nn.sigmoid(proj(5, everything))

    qn = qk_norm(proj(1), gq_ref)
    if latent:
        qn = _rope(qn, cos_ref[...], sn_ref[...], sp_ref[...])
    q_ref[...] = (qn * (QK_DIM ** -0.5 * LOG2E)).astype(BF16)

    kn = qk_norm(proj(2), gk_ref)
    if latent:
        kn = _rope(kn, cos_ref[...], sn_ref[...], sp_ref[...])
    else:
        cache_out(kf_ref, kprev_ref if stacked else None, kn)
    k_ref[...] = kn.astype(BF16)

    v = proj(3)
    if not latent:
        cache_out(vf_ref, vprev_ref if stacked else None, v)
    v_ref[...] = v.astype(BF16)
    ufour_ref[...] = proj(6).astype(BF16)

    chunks = tm // SEQ_CHUNK
    for c in range(chunks):
        lo = c * SEQ_CHUNK
        views = [scr.at[rows, :] for scr in (up_scr, glu_scr)
                 for rows in (slice(lo, lo + HALO), slice(lo + HALO, lo + HALO + SEQ_CHUNK),
                              slice(lo + HALO + SEQ_CHUNK, lo + 2 * HALO + SEQ_CHUNK))]
        y_pool, y_conv = _seqmix_chunk(pl.program_id(0) * chunks + c, chunks_per_seq, *views, *mix_refs,
                                       pool_scr, conv_scr, shift_scr, acc_scr)
        ypool_ref[lo:lo + SEQ_CHUNK, :] = y_pool
        yconv_ref[lo:lo + SEQ_CHUNK, :] = y_conv


def _inproj(x, mod, g_norm1, w_in, g_q, g_k, p, layer, seq_len, latent, tm, prev_kv=None, cast=None):
    m_rows = x.shape[0]
    w = MIXER_WIDTH
    if latent:
        row_of = lambda i: 1 + i // (seq_len // tm)
    else:
        row_of = lambda i: 0
    hb = tm // HALO
    n_halo_blocks = m_rows // HALO
    row_spec = lambda: pl.BlockSpec((tm, w), lambda i: (i, 0))
    vec_spec = lambda: pl.BlockSpec((1, w), lambda i: (0, 0))
    layer_vec = lambda a: a.reshape(DEPTH, 1, a.shape[-1])
    layer_vec_spec = lambda n=w: pl.BlockSpec((None, 1, n), lambda i: (layer, 0, 0))
    in_specs = [
        pl.BlockSpec((HALO, D_MODEL), lambda i: (jnp.maximum(i * hb - 1, 0), 0)),
        pl.BlockSpec((tm, D_MODEL), lambda i: (i, 0)),
        pl.BlockSpec((HALO, D_MODEL), lambda i: (jnp.minimum((i + 1) * hb, n_halo_blocks - 1), 0)),
        _mod_spec(layer, 0, row_of),
        _mod_spec(layer, 1, row_of),
        layer_vec_spec(D_MODEL),
        pl.BlockSpec((None, D_MODEL, IN_COLS), lambda i: (0, 0, 0), pipeline_mode=pl.Buffered(1)),
        vec_spec(), vec_spec(),
        pl.BlockSpec((w // 2, w // 2), lambda i: (0, 0)),
    ]
    args = [x, x, x, mod, mod, layer_vec(g_norm1), w_in,
            jnp.tile(g_q[layer], w // QK_DIM).reshape(1, w),
            jnp.tile(g_k[layer], w // QK_DIM).reshape(1, w),
            _group_mean_matrix()]
    if latent:
        blocks_per_seq = seq_len // tm
        for tab in _rope_tables(seq_len):
            in_specs.append(pl.BlockSpec((tm, LANES), lambda i: (i % blocks_per_seq, 0)))
            args.append(jnp.asarray(tab))
    in_specs += [pl.BlockSpec((None, len(POOL_WINDOWS), POOL_GROUP_DIM, POOL_GROUP_DIM),
                              lambda i: (layer, 0, 0, 0)),
                 layer_vec_spec(),
                 pl.BlockSpec((None, CONV_TAPS, SUBLANES, w), lambda i: (layer, 0, 0, 0)),
                 layer_vec_spec(), layer_vec_spec(), layer_vec_spec(),
                 pl.BlockSpec((None, w, w), lambda i: (layer, 0, 0)),
                 layer_vec_spec()]
    args += [p["pool_w"], layer_vec(p["pool_scale"]),
             jnp.broadcast_to(p["conv_dw"][:, :, None, :], (DEPTH, CONV_TAPS, SUBLANES, w)),
             layer_vec(p["conv_dw_b"]), layer_vec(p["conv_ln_g"]), layer_vec(p["conv_ln_b"]),
             p["conv_pw"], layer_vec(p["conv_pw_b"])]
    out_shape = [jax.ShapeDtypeStruct((m_rows, w), BF16)] * 6
    out_specs = [row_spec() for _ in range(6)]
    stacked = prev_kv is not None
    if stacked:
        assert DEPTH == 2 and layer == 1 and not latent
        seqs = tm // seq_len
        in_specs += [row_spec(), row_spec()]
        args += list(prev_kv)
        out_shape += [jax.ShapeDtypeStruct((m_rows // seq_len, DEPTH, seq_len, w), F32)] * 2
        out_specs += [pl.BlockSpec((seqs, DEPTH, seq_len, w), lambda i: (i, 0, 0, 0)) for _ in range(2)]
    elif not latent:
        out_shape += [jax.ShapeDtypeStruct((m_rows, w), F32)] * 2
        out_specs += [row_spec(), row_spec()]
    n_cast = 0
    if cast is not None:
        weights, cast_layer = cast
        for a in weights:
            rows, cols = a.shape[1:]
            br = rows // (m_rows // tm)
            in_specs.append(pl.BlockSpec((None, br, cols), lambda i: (cast_layer, i, 0)))
            out_specs.append(pl.BlockSpec((None, br, cols), lambda i: (0, i, 0)))
            out_shape.append(jax.ShapeDtypeStruct((1, rows, cols), BF16))
        args += list(weights)
        n_cast = len(weights)
    r = SEQ_CHUNK
    return pl.pallas_call(
        functools.partial(_inproj_kernel, latent, seq_len // r, stacked, n_cast),
        grid=(m_rows // tm,),
        in_specs=in_specs,
        out_specs=out_specs,
        out_shape=out_shape,
        scratch_shapes=[pltpu.VMEM((tm + 2 * HALO, D_MODEL), BF16),
                        pltpu.VMEM((tm + 2 * HALO, w), F32), pltpu.VMEM((tm + 2 * HALO, w), F32),
                        pltpu.VMEM((r + 2 * HALO, w), F32), pltpu.VMEM((r + 2 * HALO, w), F32),
                        pltpu.VMEM((SUBLANES - 1, r + SUBLANES * (CONV_TAPS // SUBLANES), w), F32),
                        pltpu.VMEM((r, w), F32)],
        compiler_params=_params("arbitrary"),
        name="inproj_lat" if latent else "inproj_ctx",
    )(*args)


def _shift_rows(x, s):
    return pltpu.roll(x, s % x.shape[0], axis=0)


def _seqmix_chunk(i, chunks_per_seq, up_prev, up_cur, up_next, gl_prev, gl_cur, gl_next,
                  pw_ref, ps_ref, dw_ref, dwb_ref, lng_ref, lnb_ref, cw_ref, cwb_ref,
                  pool_scr, conv_scr, shift_scr, acc_scr):
    r = SEQ_CHUNK
    first = (i % chunks_per_seq) == 0
    last = (i % chunks_per_seq) == chunks_per_seq - 1
    zero_halo = jnp.zeros((HALO, MIXER_WIDTH), F32)

    def fill(scr, prev, cur, nxt):
        scr[0:HALO, :] = jnp.where(first, zero_halo, prev[...])
        scr[HALO:HALO + r, :] = cur[...]
        scr[HALO + r:2 * HALO + r, :] = jnp.where(last, zero_halo, nxt[...])

    fill(pool_scr, up_prev, up_cur, up_next)
    fill(conv_scr, gl_prev, gl_cur, gl_next)

    row = lax.broadcasted_iota(jnp.int32, (r, 1), 0) + (i % chunks_per_seq) * r
    seq_len = chunks_per_seq * r
    outs = []
    for g, win in enumerate(POOL_WINDOWS):
        lanes = slice(g * POOL_GROUP_DIM, (g + 1) * POOL_GROUP_DIM)
        u = pool_scr[:, lanes]
        s = _shift_rows(u, 1) + u
        span = 2
        while span < win:
            s = _shift_rows(s, span // 2) + _shift_rows(s, -(span // 2))
            span *= 2
        lo = jnp.maximum(row - win // 2, 0)
        hi = jnp.minimum(row + win // 2 - 1, seq_len - 1)
        cnt = (hi - lo + 1).astype(F32)
        p = s[HALO:HALO + r] / cnt - u[HALO:HALO + r]
        outs.append(jnp.dot(p.astype(BF16), pw_ref[g], preferred_element_type=F32))
    y_pool = (jnp.concatenate(outs, axis=1) * ps_ref[...]).astype(BF16)

    shifted_rows = r + SUBLANES * (CONV_TAPS // SUBLANES)
    for lo in range(1, SUBLANES):
        shift_scr[lo - 1] = conv_scr[lo:lo + shifted_rows, :]
    groups = CONV_SUB // SUBLANES
    for c in range(r // CONV_SUB):
        acc = jnp.zeros((groups, SUBLANES, MIXER_WIDTH), F32)
        for j in range(CONV_TAPS):
            hi, lo = divmod(j + 1, SUBLANES)
            src = conv_scr if lo == 0 else shift_scr.at[lo - 1]
            start = c * CONV_SUB + SUBLANES * hi
            rows = src[start:start + CONV_SUB, :].reshape(groups, SUBLANES, MIXER_WIDTH)
            acc = acc + rows * dw_ref[j]
        acc_scr[c * CONV_SUB:(c + 1) * CONV_SUB, :] = acc.reshape(CONV_SUB, MIXER_WIDTH)
    y = acc_scr[...] + dwb_ref[...]
    mu = jnp.mean(y, axis=-1, keepdims=True)
    yc = y - mu
    yn = yc * lax.rsqrt(jnp.mean(yc * yc, axis=-1, keepdims=True) + EPS) * lng_ref[...] + lnb_ref[...]
    act = yn * jax.nn.sigmoid(yn)
    y_conv = (jnp.dot(act.astype(BF16), cw_ref[...], preferred_element_type=F32)
              + cwb_ref[...]).astype(BF16)
    return y_pool, y_conv


def _outproj_kernel(x_ref, p0, p1, p2, p3, w_ref, g1_ref, sh_ref, sc_ref, gn_ref, x1_ref, h2_ref):
    half = x_ref.shape[0] // 2
    for rows in (slice(0, half), slice(half, 2 * half)):
        mix = jnp.concatenate([p0[rows, :], p1[rows, :], p2[rows, :], p3[rows, :]], axis=1)
        a = jnp.dot(mix, w_ref[...], preferred_element_type=F32)
        x1 = x_ref[rows, :] + g1_ref[...] * a
        x1_ref[rows, :] = x1
        h2_ref[rows, :] = (_rms(x1, gn_ref[...]) * (1.0 + sc_ref[...]) + sh_ref[...]).astype(BF16)


def _outproj(x, parts, w_out, mod, g_norm2, layer, row_of, tm, name):
    m_rows = x.shape[0]
    w = MIXER_WIDTH
    row_spec = lambda: pl.BlockSpec((tm, D_MODEL), lambda i: (i, 0))
    part_spec = lambda: pl.BlockSpec((tm, w), lambda i: (i, 0))
    return pl.pallas_call(
        _outproj_kernel,
        grid=(m_rows // tm,),
        in_specs=[row_spec(), part_spec(), part_spec(), part_spec(), part_spec(),
                  pl.BlockSpec((None, D_MODEL, D_MODEL), lambda i: (0, 0, 0), pipeline_mode=pl.Buffered(1)),
                  _mod_spec(layer, 2, row_of), _mod_spec(layer, 3, row_of), _mod_spec(layer, 4, row_of),
                  pl.BlockSpec((None, 1, D_MODEL), lambda i: (layer, 0, 0))],
        out_specs=[row_spec(), row_spec()],
        out_shape=[jax.ShapeDtypeStruct((m_rows, D_MODEL), F32),
                   jax.ShapeDtypeStruct((m_rows, D_MODEL), BF16)],
        compiler_params=_params("arbitrary"),
        name=name,
    )(x, *parts, w_out, mod, mod, mod, g_norm2.reshape(DEPTH, 1, D_MODEL))


def _fourier_kernel(seq_len, scale, u_ref, pos_ref, ch_ref, fw_ref, o_ref, ab_scr, pos_scr):
    b = pl.program_id(1)

    @pl.when(b == 0)
    def _():
        pos_scr[...] = pos_ref[...].astype(BF16)

    @pl.when(pl.program_id(0) == 0)
    def _():
        ch = ch_ref[...].astype(BF16)
        for h in range(MIXER_WIDTH // FOURIER_HEAD_DIM):
            lanes = slice(h * FOURIER_HEAD_DIM, (h + 1) * FOURIER_HEAD_DIM)
            a = jnp.dot(u_ref[:, lanes], ch, preferred_element_type=F32)
            ab_scr[b, 0:seq_len, lanes] = a[:, :FOURIER_HEAD_DIM].astype(BF16)
            ab_scr[b, seq_len:2 * seq_len, lanes] = a[:, FOURIER_HEAD_DIM:].astype(BF16)

    f = jnp.dot(pos_scr[...], ab_scr[b], preferred_element_type=F32) * scale
    o_ref[...] = jnp.dot(f.astype(BF16), fw_ref[...], preferred_element_type=F32).astype(BF16)


def _fourier(ufour, fourier_w, layer, seq_len, name):
    m_rows = ufour.shape[0]
    w = MIXER_WIDTH
    batch = m_rows // seq_len
    tl = min(seq_len, 512)
    blocks = seq_len // tl
    pos_tab, ch_tab = _dft_tables(seq_len)
    scale = float(1.0 / np.sqrt(seq_len * FOURIER_HEAD_DIM))
    return pl.pallas_call(
        functools.partial(_fourier_kernel, seq_len, scale),
        grid=(blocks, batch),
        in_specs=[pl.BlockSpec((seq_len, w), lambda j, b: (jnp.where(j == 0, b, batch - 1), 0)),
                  pl.BlockSpec((tl, 2 * seq_len), lambda j, b: (j, 0)),
                  pl.BlockSpec((FOURIER_HEAD_DIM, 2 * FOURIER_HEAD_DIM), lambda j, b: (0, 0)),
                  pl.BlockSpec((None, w, w), lambda j, b: (layer, 0, 0))],
        out_specs=pl.BlockSpec((tl, w), lambda j, b: (b * blocks + j, 0)),
        out_shape=jax.ShapeDtypeStruct((m_rows, w), BF16),
        scratch_shapes=[pltpu.VMEM((batch, 2 * seq_len, w), BF16), pltpu.VMEM((tl, 2 * seq_len), BF16)],
        compiler_params=_params("arbitrary", "arbitrary"),
        name=name,
    )(ufour, pos_tab, ch_tab, fourier_w)


def _cast_blocks(refs):
    half = len(refs) // 2
    for src, dst in zip(refs[:half], refs[half:]):
        if len(dst.shape) == len(src.shape):
            dst[...] = src[...].astype(BF16)
        else:
            for j in range(dst.shape[0]):
                dst[j] = src[:, j * FFN_TF:(j + 1) * FFN_TF].astype(BF16)


def _ffn_weight_cast_specs(weights, layer, n_row, n_col, row_col_of):
    br, bc = D_MODEL // n_row, D_FF // n_col
    slabs = bc // FFN_TF

    def up_map(*ids):
        r, c = row_col_of(*ids)
        return (layer, r, c)

    def down_map(*ids):
        r, c = row_col_of(*ids)
        return (layer, c, r)

    def tiled_map(*ids):
        r, c = row_col_of(*ids)
        return (0, c, r, 0)

    in_specs = [pl.BlockSpec((None, br, bc), up_map), pl.BlockSpec((None, br, bc), up_map),
                pl.BlockSpec((None, bc, br), down_map)]
    out_specs = [pl.BlockSpec((None, slabs, br, FFN_TF), tiled_map),
                 pl.BlockSpec((None, slabs, br, FFN_TF), tiled_map),
                 pl.BlockSpec((None, bc, br), lambda *ids: (0,) + down_map(*ids)[1:])]
    tiled = jax.ShapeDtypeStruct((1, D_FF // FFN_TF, D_MODEL, FFN_TF), BF16)
    out_shape = [tiled, tiled, jax.ShapeDtypeStruct((1, D_FF, D_MODEL), BF16)]
    return in_specs, out_specs, out_shape


def _attn_kernel(lam_init, cached, heads, n_cast, *refs):
    n_in = (7 if cached else 5) + n_cast
    cast_src, cast_dst = refs[n_in - n_cast:n_in], refs[n_in + 1:n_in + 1 + n_cast]
    o_ref = refs[n_in]
    scratch = refs[n_in + 1 + n_cast:]
    if cached:
        q_ref, k_ref, v_ref, ck_ref, cv_ref, lam_ref, gs_ref = refs[:7]
        vt_scr, s_scr, ckb_scr = scratch
    else:
        q_ref, k_ref, v_ref, lam_ref, gs_ref = refs[:5]
        vt_scr, s_scr = scratch
    _cast_blocks(cast_src + cast_dst)
    tq = q_ref.shape[0]
    own = k_ref.shape[0]
    past = ck_ref.shape[0] if cached else 0

    @pl.when(pl.program_id(2) == 0)
    def _():
        for h in range(heads):
            cols = slice(h * V_DIM, (h + 1) * V_DIM)
            base = h * VT_ROWS
            vt_scr[base:base + V_DIM, 0:own] = v_ref[:, cols].astype(F32).T.astype(BF16)
            vt_scr[base + V_DIM:base + VT_ROWS, :] = jnp.ones((VT_ROWS - V_DIM, own + past), BF16)
            if cached:
                vt_scr[base:base + V_DIM, own:] = cv_ref[:, cols].T.astype(BF16)
                ckb_scr[:, cols] = ck_ref[:, cols].astype(BF16)

    lv = lam_ref[...]
    lam = (jnp.exp(jnp.sum(lv[0:1] * lv[1:2], axis=-1, keepdims=True))
           - jnp.exp(jnp.sum(lv[2:3] * lv[3:4], axis=-1, keepdims=True)) + lam_init)
    lane = lax.broadcasted_iota(jnp.int32, (1, V_DIM), 1)
    nt = (((1,), (1,)), ((), ()))
    segments = [(k_ref, 0, own)] + ([(ckb_scr, own, past)] if cached else [])
    items = [(h, key_ref, offset, c0, min(KEY_CHUNK, count - c0))
             for h in range(heads) for key_ref, offset, count in segments
             for c0 in range(0, count, KEY_CHUNK)]
    qqs = []
    for h in range(heads):
        q = q_ref[:, h * V_DIM:(h + 1) * V_DIM]
        zero = jnp.zeros_like(q)
        qqs.append(jnp.concatenate([jnp.where(lane < QK_DIM, q, zero),
                                    jnp.where(lane >= QK_DIM, q, zero)], axis=0))

    def scores(n):
        h, key_ref, _, c0, kc = items[n]
        s_scr[n % 2, 0:kc, :] = lax.dot_general(key_ref[c0:c0 + kc, h * V_DIM:(h + 1) * V_DIM], qqs[h], nt,
                                                preferred_element_type=F32)

    scores(0)
    parts = [[] for _ in range(heads)]
    for n, (h, _, offset, c0, kc) in enumerate(items):
        if n + 1 < len(items):
            scores(n + 1)
        s = s_scr[n % 2, 0:kc, :]
        mc = s.max(axis=0, keepdims=True)
        e = jnp.exp2(s - mc).astype(BF16)
        oc = jnp.dot(vt_scr[h * VT_ROWS:(h + 1) * VT_ROWS, offset + c0:offset + c0 + kc], e,
                     preferred_element_type=F32)
        parts[h].append((mc, oc))
    for h in range(heads):
        m = functools.reduce(jnp.maximum, [mc for mc, _ in parts[h]])
        acc = functools.reduce(jnp.add, [oc * jnp.exp2(mc - m) for mc, oc in parts[h]])
        inv = 1.0 / acc[V_DIM:V_DIM + 1]
        o_t = acc[:V_DIM, :tq] * inv[:, :tq] - lam * (acc[:V_DIM, tq:] * inv[:, tq:])
        o_ref[:, h * V_DIM:(h + 1) * V_DIM] = (_rms(o_t.T, gs_ref[...]) * (1.0 - lam_init)).astype(BF16)


def _attention(q, k, v, cache_k, cache_v, lam, g_subln, layer, lam_init, seq_len, tq, heads, cast, name):
    m_rows = q.shape[0]
    batch = m_rows // seq_len
    qb = seq_len // tq
    cached = cache_k is not None
    hw = heads * V_DIM
    in_specs = [pl.BlockSpec((tq, hw), lambda b, h, i: (b * qb + i, h)),
                pl.BlockSpec((seq_len, hw), lambda b, h, i: (b, h)),
                pl.BlockSpec((seq_len, hw), lambda b, h, i: (b, h))]
    args = [q, k, v]
    past = 0
    scratch = []
    if cached:
        past = cache_k.shape[2]
        spec = lambda: pl.BlockSpec((None, None, past, hw), lambda b, h, i: (b, layer, 0, h))
        in_specs += [spec(), spec()]
        args += [cache_k, cache_v]
        scratch = [pltpu.VMEM((past, hw), BF16)]
    in_specs += [pl.BlockSpec((None, 4, QK_DIM), lambda b, h, i: (layer, 0, 0)),
                 pl.BlockSpec((None, 1, V_DIM), lambda b, h, i: (layer, 0, 0))]
    args += [lam, g_subln.reshape(DEPTH, 1, V_DIM)]
    groups = N_HEADS // heads
    out_specs = [pl.BlockSpec((tq, hw), lambda b, h, i: (b * qb + i, h))]
    out_shape = [jax.ShapeDtypeStruct((m_rows, N_HEADS * V_DIM), BF16)]
    n_cast = 0
    if cast is not None:
        weights, cast_layer = cast
        assert groups == 1
        ci, co, cs = _ffn_weight_cast_specs(weights, cast_layer, batch * qb, 1,
                                            lambda b, h, i: (b * qb + i, 0))
        in_specs += ci
        out_specs += co
        out_shape += cs
        args += list(weights)
        n_cast = len(weights)
    outs = pl.pallas_call(
        functools.partial(_attn_kernel, lam_init, cached, heads, n_cast),
        grid=(batch, groups, qb),
        in_specs=in_specs,
        out_specs=out_specs,
        out_shape=out_shape,
        scratch_shapes=[pltpu.VMEM((heads * VT_ROWS, seq_len + past), BF16),
                        pltpu.VMEM((2, min(KEY_CHUNK, seq_len), 2 * tq), F32)] + scratch,
        compiler_params=_params("arbitrary", "arbitrary", "arbitrary"),
        name=name,
    )(*args)
    return outs[0], tuple(outs[1:])


def _ffn_kernel(h_ref, x_ref, wg_ref, wu_ref, wd_ref, g2_ref, o_ref, acc_scr):
    f = pl.program_id(1)
    last = pl.num_programs(1) - 1

    def partial_down():
        h = h_ref[...]
        g = jnp.dot(h, wg_ref[...], preferred_element_type=F32)
        u = jnp.dot(h, wu_ref[...], preferred_element_type=F32)
        a = (g * jax.nn.sigmoid(g) * u).astype(BF16)
        return jnp.dot(a, wd_ref[...], preferred_element_type=F32)

    @pl.when(f == 0)
    def _():
        acc_scr[...] = partial_down()

    @pl.when((f > 0) & (f < last))
    def _():
        acc_scr[...] += partial_down()

    @pl.when(f == last)
    def _():
        o_ref[...] = x_ref[...] + g2_ref[...] * (acc_scr[...] + partial_down())


def _ffn(h2, x1, ffn_w, mod, layer, row_of, tm, name):
    tf = FFN_TF
    m_rows = x1.shape[0]
    return pl.pallas_call(
        _ffn_kernel,
        grid=(m_rows // tm, D_FF // tf),
        in_specs=[pl.BlockSpec((tm, D_MODEL), lambda i, f: (i, 0)),
                  pl.BlockSpec((tm, D_MODEL), lambda i, f: (i, 0)),
                  pl.BlockSpec((None, None, D_MODEL, tf), lambda i, f: (0, f, 0, 0)),
                  pl.BlockSpec((None, None, D_MODEL, tf), lambda i, f: (0, f, 0, 0)),
                  pl.BlockSpec((None, tf, D_MODEL), lambda i, f: (0, f, 0)),
                  _mod_spec(layer, 5, row_of)],
        out_specs=pl.BlockSpec((tm, D_MODEL), lambda i, f: (i, 0)),
        out_shape=jax.ShapeDtypeStruct((m_rows, D_MODEL), F32),
        scratch_shapes=[pltpu.VMEM((tm, D_MODEL), F32)],
        compiler_params=_params("arbitrary", "arbitrary"),
        name=name,
    )(h2, x1, *ffn_w, mod)


def kernel(x_prompt, x_sample, cache_k, cache_v, c, c_ctx, w_ada, b_ada, g_norm1, w_in, pool_w, pool_scale, g_q, g_k, lam, g_subln, conv_dw, conv_dw_b, conv_ln_g, conv_ln_b, conv_pw, conv_pw_b, fourier_w, w_out, g_norm2, w_gate, w_up, w_down):
    batch, seq, _ = x_prompt.shape
    dec_batch, dec_seq, _ = x_sample.shape
    past = cache_k.shape[2]
    tm = 512

    cc = jnp.concatenate([c_ctx[None, :], c, jnp.zeros((8 - 1 - dec_batch, D_MODEL), F32)], axis=0)
    mod = _modulation(cc, w_ada, b_ada).reshape(DEPTH, 8, 1, 6 * D_MODEL)

    bf = lambda a: a.astype(BF16)
    fourier_w_b = bf(fourier_w)
    mix_params = dict(pool_w=bf(pool_w), pool_scale=pool_scale, conv_dw=conv_dw, conv_dw_b=conv_dw_b,
                      conv_ln_g=conv_ln_g, conv_ln_b=conv_ln_b, conv_pw=bf(conv_pw), conv_pw_b=conv_pw_b)
    ffn_f32 = (w_gate, w_up, w_down)
    ck = cache_k.reshape(dec_batch, DEPTH, past, N_HEADS * V_DIM)
    cv = cache_v.reshape(dec_batch, DEPTH, past, N_HEADS * V_DIM)

    streams = {
        "lat": dict(x=x_sample.reshape(dec_batch * dec_seq, D_MODEL), seq=dec_seq, latent=True,
                    row_of=lambda i: 1 + i // (dec_seq // tm)),
        "ctx": dict(x=x_prompt.reshape(batch * seq, D_MODEL), seq=seq, latent=False,
                    row_of=lambda i: 0),
    }
    cache_kv = None
    ffn_b = None
    proj_b = (bf(w_in[0:1]), bf(w_out[0:1]))
    proj_next = None
    for layer in range(DEPTH):
        lam_init = 0.8 - 0.6 * float(np.exp(-0.3 * layer))
        if layer > 0:
            proj_b = proj_next
        for name, st in streams.items():
            latent, seq_len = st["latent"], st["seq"]
            cast = ((w_in, w_out), layer + 1) if (latent and layer + 1 < DEPTH) else None
            outs = _inproj(st["x"], mod, g_norm1, proj_b[0], g_q, g_k, mix_params, layer, seq_len, latent, tm,
                           prev_kv=cache_kv if (not latent and layer > 0) else None, cast=cast)
            if cast is not None:
                proj_next = tuple(outs[-2:])
            y_pool, q, k, v, y_conv, ufour = outs[:6]
            if not latent:
                cache_kv = (outs[6], outs[7])
            y_four = _fourier(ufour, fourier_w_b, layer, seq_len, "fourier_" + name)
            att, cast_out = _attention(q, k, v, ck if latent else None, cv if latent else None, lam, g_subln,
                                       layer, lam_init, seq_len, 512 if latent else seq_len, N_HEADS,
                                       (ffn_f32, layer) if latent else None, "attn_" + name)
            if cast_out:
                ffn_b = cast_out
            x1, h2 = _outproj(st["x"], (y_pool, att, y_conv, y_four), proj_b[1], mod, g_norm2, layer,
                              st["row_of"], tm, "outproj_" + name)
            st["x"] = _ffn(h2, x1, ffn_b, mod, layer, st["row_of"], tm, "ffn_" + name)
    y_prompt = streams["ctx"]["x"].reshape(batch, seq, D_MODEL)
    y_sample = streams["lat"]["x"].reshape(dec_batch, dec_seq, D_MODEL)
    new_k = cache_kv[0].reshape(batch, DEPTH, seq, N_HEADS, 2, QK_DIM)
    new_v = cache_kv[1].reshape(batch, DEPTH, seq, N_HEADS, V_DIM)
    return (y_prompt, y_sample, new_k, new_v)
```

```python
import functools

import numpy as np
import jax
import jax.numpy as jnp
from jax import lax
from jax.experimental import pallas as pl
from jax.experimental.pallas import tpu as pltpu

D_MODEL = 2048
DEPTH = 2
GRID_W = 64
MIXER_WIDTH = 512
POOL_GROUP_DIM = 128
POOL_WINDOWS = (2, 4, 8, 16)
N_HEADS = 4
V_DIM = 128
QK_DIM = 64
ROPE_AXIS_DIM = 32
ROPE_BASE = 10000.0
CONV_TAPS = 31
FOURIER_HEAD_DIM = 128
IN_COLS = 7 * MIXER_WIDTH
D_FF = 5632
EPS = 1e-6

LANES = 128
SUBLANES = 8
HALO = 16
SEQ_CHUNK = 256
CONV_SUB = 32
KEY_CHUNK = 512
FFN_TF = 512
VT_ROWS = V_DIM + 2 * SUBLANES
LOG2E = 1.4426950408889634
VMEM_LIMIT = 56 * 1024 * 1024

BF16 = jnp.bfloat16
F32 = jnp.float32


def _params(*semantics):
    return pltpu.CompilerParams(dimension_semantics=semantics, vmem_limit_bytes=VMEM_LIMIT)


@functools.lru_cache(maxsize=None)
def _rope_tables(seq_len):
    t = np.arange(seq_len)
    pos = np.stack([(t // GRID_W), (t % GRID_W)], axis=0).astype(np.float64)
    half = ROPE_AXIS_DIM // 2
    inv = 1.0 / (ROPE_BASE ** (np.arange(0, ROPE_AXIS_DIM, 2, dtype=np.float64) / ROPE_AXIS_DIM))
    lane = np.arange(LANES)
    d = lane % QK_DIM
    axis = d // ROPE_AXIS_DIM
    freq = d % half
    ang = pos[axis, :].T * inv[freq][None, :]
    first = ((d % ROPE_AXIS_DIM) < half)[None, :]
    cos = np.cos(ang)
    sin = np.sin(ang)
    sin_next = np.where(first, -sin, 0.0)
    sin_prev = np.where(first, 0.0, sin)
    return (cos.astype(np.float32), sin_next.astype(np.float32), sin_prev.astype(np.float32))


@functools.lru_cache(maxsize=None)
def _dft_tables(seq_len):
    def cs(n):
        k = np.arange(n)
        m = (k[:, None] * k[None, :]) % n
        a = 2.0 * np.pi * m.astype(np.float64) / n
        return np.cos(a), np.sin(a)
    cl, sl = cs(seq_len)
    cc, sc = cs(FOURIER_HEAD_DIM)
    pos = np.concatenate([cl, -sl], axis=1).astype(np.float32)
    ch = np.concatenate([cc, sc], axis=1).astype(np.float32)
    return pos, ch


@functools.lru_cache(maxsize=None)
def _group_mean_matrix():
    g = np.arange(MIXER_WIDTH // 2) // QK_DIM
    p = (g[:, None] == g[None, :]).astype(np.float32) / QK_DIM
    return jnp.asarray(p, dtype=BF16)


def _mod_kernel(c_ref, w_ref, b_ref, o_ref):
    c = c_ref[...]
    s = (c * jax.nn.sigmoid(c)).astype(BF16)
    o_ref[...] = jnp.dot(s, w_ref[...].astype(BF16), preferred_element_type=F32) + b_ref[...]


def _modulation(cc, w_ada, b_ada):
    tn = 1024
    n_cols = 6 * D_MODEL
    return pl.pallas_call(
        _mod_kernel,
        grid=(DEPTH, n_cols // tn),
        in_specs=[
            pl.BlockSpec((8, D_MODEL), lambda l, j: (0, 0)),
            pl.BlockSpec((None, D_MODEL, tn), lambda l, j: (l, 0, j)),
            pl.BlockSpec((None, 1, tn), lambda l, j: (l, 0, j)),
        ],
        out_specs=pl.BlockSpec((None, 8, tn), lambda l, j: (l, 0, j)),
        out_shape=jax.ShapeDtypeStruct((DEPTH, 8, n_cols), F32),
        compiler_params=_params("arbitrary", "arbitrary"),
        name="adaln_mod",
    )(cc, w_ada, b_ada.reshape(DEPTH, 1, n_cols))


def _mod_spec(layer, chunk, row_of_block):
    return pl.BlockSpec((None, None, 1, D_MODEL),
                        lambda i, *_: (layer, row_of_block(i), 0, chunk))


def _rms(x, gain):
    return x * lax.rsqrt(jnp.mean(x * x, axis=-1, keepdims=True) + EPS) * gain


def _rope(t, cos, sin_next, sin_prev):
    outs = []
    for c in range(MIXER_WIDTH // LANES):
        tc = t[:, c * LANES:(c + 1) * LANES]
        outs.append(tc * cos
                    + pltpu.roll(tc, LANES - ROPE_AXIS_DIM // 2, axis=1) * sin_next
                    + pltpu.roll(tc, ROPE_AXIS_DIM // 2, axis=1) * sin_prev)
    return jnp.concatenate(outs, axis=1)


def _inproj_kernel(latent, chunks_per_seq, stacked, n_cast, *refs):
    xp_ref, x_ref, xn_ref, sh_ref, sc_ref, g1_ref, w_ref, gq_ref, gk_ref, p_ref = refs[:10]
    n_in = 10
    if latent:
        cos_ref, sn_ref, sp_ref = refs[n_in:n_in + 3]
        n_in += 3
    mix_refs = refs[n_in:n_in + 8]
    n_in += 8
    if stacked:
        kprev_ref, vprev_ref = refs[n_in:n_in + 2]
        n_in += 2
    cast_src = refs[n_in:n_in + n_cast]
    n_in += n_cast

    def cache_out(ref, prev_ref, new, rows):
        if stacked:
            _, _, seq, width = ref.shape
            first, count = rows.start // seq, (rows.stop - rows.start) // seq
            ref[first:first + count, 0] = prev_ref[rows, :].reshape(count, seq, width)
            ref[first:first + count, 1] = new.reshape(count, seq, width)
        else:
            ref[rows, :] = new
    ypool_ref, q_ref, k_ref, v_ref, yconv_ref, ufour_ref = refs[n_in:n_in + 6]
    n_out = 6
    if not latent:
        kf_ref, vf_ref = refs[n_in + 6:n_in + 8]
        n_out = 8
    cast_dst = refs[n_in + n_out:n_in + n_out + n_cast]
    n_out += n_cast
    h_scr, up_scr, glu_scr, pool_scr, conv_scr, shift_scr, acc_scr = refs[n_in + n_out:]
    _cast_blocks(cast_src + cast_dst)
    w = MIXER_WIDTH
    tm = x_ref.shape[0]
    own = slice(HALO, HALO + tm)

    def mod_norm(x):
        return (_rms(x, g1_ref[...]) * (1.0 + sc_ref[...]) + sh_ref[...]).astype(BF16)

    h_scr[0:HALO, :] = mod_norm(xp_ref[...])
    h_scr[own, :] = mod_norm(x_ref[...])
    h_scr[HALO + tm:, :] = mod_norm(xn_ref[...])

    def in_tile(rows):
        return slice(HALO + rows.start, HALO + rows.stop)

    def proj(n, rows):
        return jnp.dot(h_scr[rows, :], w_ref[:, n * w:(n + 1) * w], preferred_element_type=F32)

    def qk_norm(z, gain_ref):
        zz = (z * z).astype(BF16)
        half = p_ref.shape[0]
        ms = jnp.concatenate([jnp.dot(zz[:, :half], p_ref[...], preferred_element_type=F32),
                              jnp.dot(zz[:, half:], p_ref[...], preferred_element_type=F32)], axis=1)
        return z * lax.rsqrt(ms + EPS) * gain_ref[...]

    everything = slice(0, tm + 2 * HALO)
    up_scr[...] = proj(0, everything)
    glu_scr[...] = proj(4, everything) * jax.nn.sigmoid(proj(5, everything))

    def q_part(rows):
        qn = qk_norm(proj(1, in_tile(rows)), gq_ref)
        if latent:
            qn = _rope(qn, cos_ref[rows, :], sn_ref[rows, :], sp_ref[rows, :])
        q_ref[rows, :] = (qn * (QK_DIM ** -0.5 * LOG2E)).astype(BF16)
        return qn[0:SUBLANES, 0:LANES]

    def k_part(rows):
        kn = qk_norm(proj(2, in_tile(rows)), gk_ref)
        if latent:
            kn = _rope(kn, cos_ref[rows, :], sn_ref[rows, :], sp_ref[rows, :])
        else:
            cache_out(kf_ref, kprev_ref if stacked else None, kn, rows)
        k_ref[rows, :] = kn.astype(BF16)
        return kn[0:SUBLANES, 0:LANES]

    def v_part(rows):
        v = proj(3, in_tile(rows))
        if not latent:
            cache_out(vf_ref, vprev_ref if stacked else None, v, rows)
        v_ref[rows, :] = v.astype(BF16)
        return v[0:SUBLANES, 0:LANES]

    def four_part(rows):
        f = proj(6, in_tile(rows))
        ufour_ref[rows, :] = f.astype(BF16)
        return f[0:SUBLANES, 0:LANES]

    chunks = tm // SEQ_CHUNK
    pieces = SEQ_CHUNK // CONV_SUB
    whole = slice(0, tm)
    parts = [functools.partial(part, whole) for part in (q_part, k_part, v_part, four_part)]
    pieces_per_part = chunks * pieces // len(parts)
    for c in range(chunks):
        def seed(piece, c=c):
            n = c * pieces + piece
            if (n + 1) % pieces_per_part:
                return None
            return _zero_of(parts[n // pieces_per_part]())
        lo = c * SEQ_CHUNK
        views = [scr.at[rows, :] for scr in (up_scr, glu_scr)
                 for rows in (slice(lo, lo + HALO), slice(lo + HALO, lo + HALO + SEQ_CHUNK),
                              slice(lo + HALO + SEQ_CHUNK, lo + 2 * HALO + SEQ_CHUNK))]
        y_pool, y_conv = _seqmix_chunk(pl.program_id(0) * chunks + c, chunks_per_seq, *views, *mix_refs,
                                       pool_scr, conv_scr, shift_scr, acc_scr, seed)
        ypool_ref[lo:lo + SEQ_CHUNK, :] = y_pool
        yconv_ref[lo:lo + SEQ_CHUNK, :] = y_conv


def _inproj(x, mod, g_norm1, w_in, g_q, g_k, p, layer, seq_len, latent, tm, prev_kv=None, cast=None):
    m_rows = x.shape[0]
    w = MIXER_WIDTH
    if latent:
        row_of = lambda i: 1 + i // (seq_len // tm)
    else:
        row_of = lambda i: 0
    hb = tm // HALO
    n_halo_blocks = m_rows // HALO
    row_spec = lambda: pl.BlockSpec((tm, w), lambda i: (i, 0))
    vec_spec = lambda: pl.BlockSpec((1, w), lambda i: (0, 0))
    layer_vec = lambda a: a.reshape(DEPTH, 1, a.shape[-1])
    layer_vec_spec = lambda n=w: pl.BlockSpec((None, 1, n), lambda i: (layer, 0, 0))
    in_specs = [
        pl.BlockSpec((HALO, D_MODEL), lambda i: (jnp.maximum(i * hb - 1, 0), 0)),
        pl.BlockSpec((tm, D_MODEL), lambda i: (i, 0)),
        pl.BlockSpec((HALO, D_MODEL), lambda i: (jnp.minimum((i + 1) * hb, n_halo_blocks - 1), 0)),
        _mod_spec(layer, 0, row_of),
        _mod_spec(layer, 1, row_of),
        layer_vec_spec(D_MODEL),
        pl.BlockSpec((None, D_MODEL, IN_COLS), lambda i: (0, 0, 0), pipeline_mode=pl.Buffered(1)),
        vec_spec(), vec_spec(),
        pl.BlockSpec((w // 2, w // 2), lambda i: (0, 0)),
    ]
    args = [x, x, x, mod, mod, layer_vec(g_norm1), w_in,
            jnp.tile(g_q[layer], w // QK_DIM).reshape(1, w),
            jnp.tile(g_k[layer], w // QK_DIM).reshape(1, w),
            _group_mean_matrix()]
    if latent:
        blocks_per_seq = seq_len // tm
        for tab in _rope_tables(seq_len):
            in_specs.append(pl.BlockSpec((tm, LANES), lambda i: (i % blocks_per_seq, 0)))
            args.append(jnp.asarray(tab))
    in_specs += [pl.BlockSpec((None, len(POOL_WINDOWS), POOL_GROUP_DIM, POOL_GROUP_DIM),
                              lambda i: (layer, 0, 0, 0)),
                 layer_vec_spec(),
                 pl.BlockSpec((None, CONV_TAPS, SUBLANES, w), lambda i: (layer, 0, 0, 0)),
                 layer_vec_spec(), layer_vec_spec(), layer_vec_spec(),
                 pl.BlockSpec((None, w, w), lambda i: (layer, 0, 0)),
                 layer_vec_spec()]
    args += [p["pool_w"], layer_vec(p["pool_scale"]),
             jnp.broadcast_to(p["conv_dw"][:, :, None, :], (DEPTH, CONV_TAPS, SUBLANES, w)),
             layer_vec(p["conv_dw_b"]), layer_vec(p["conv_ln_g"]), layer_vec(p["conv_ln_b"]),
             p["conv_pw"], layer_vec(p["conv_pw_b"])]
    out_shape = [jax.ShapeDtypeStruct((m_rows, w), BF16)] * 6
    out_specs = [row_spec() for _ in range(6)]
    stacked = prev_kv is not None
    if stacked:
        assert DEPTH == 2 and layer == 1 and not latent
        seqs = tm // seq_len
        in_specs += [row_spec(), row_spec()]
        args += list(prev_kv)
        out_shape += [jax.ShapeDtypeStruct((m_rows // seq_len, DEPTH, seq_len, w), F32)] * 2
        out_specs += [pl.BlockSpec((seqs, DEPTH, seq_len, w), lambda i: (i, 0, 0, 0)) for _ in range(2)]
    elif not latent:
        out_shape += [jax.ShapeDtypeStruct((m_rows, w), F32)] * 2
        out_specs += [row_spec(), row_spec()]
    n_cast = 0
    if cast is not None:
        weights, cast_layer = cast
        for a in weights:
            rows, cols = a.shape[1:]
            br = rows // (m_rows // tm)
            in_specs.append(pl.BlockSpec((None, br, cols), lambda i: (cast_layer, i, 0)))
            out_specs.append(pl.BlockSpec((None, br, cols), lambda i: (0, i, 0)))
            out_shape.append(jax.ShapeDtypeStruct((1, rows, cols), BF16))
        args += list(weights)
        n_cast = len(weights)
    r = SEQ_CHUNK
    return pl.pallas_call(
        functools.partial(_inproj_kernel, latent, seq_len // r, stacked, n_cast),
        grid=(m_rows // tm,),
        in_specs=in_specs,
        out_specs=out_specs,
        out_shape=out_shape,
        scratch_shapes=[pltpu.VMEM((tm + 2 * HALO, D_MODEL), BF16),
                        pltpu.VMEM((tm + 2 * HALO, w), F32), pltpu.VMEM((tm + 2 * HALO, w), F32),
                        pltpu.VMEM((r + 2 * HALO, w), F32), pltpu.VMEM((r + 2 * HALO, w), F32),
                        pltpu.VMEM((SUBLANES - 1, r + SUBLANES * (CONV_TAPS // SUBLANES), w), F32),
                        pltpu.VMEM((r, w), F32)],
        compiler_params=_params("arbitrary"),
        name="inproj_lat" if latent else "inproj_ctx",
    )(*args)


def _zero_of(tile):
    u = lax.bitcast_convert_type(tile, jnp.uint32)
    u = lax.shift_right_logical(lax.shift_right_logical(u, jnp.uint32(16)), jnp.uint32(16))
    return lax.bitcast_convert_type(u, F32)


def _shift_rows(x, s):
    return pltpu.roll(x, s % x.shape[0], axis=0)


def _seqmix_chunk(i, chunks_per_seq, up_prev, up_cur, up_next, gl_prev, gl_cur, gl_next,
                  pw_ref, ps_ref, dw_ref, dwb_ref, lng_ref, lnb_ref, cw_ref, cwb_ref,
                  pool_scr, conv_scr, shift_scr, acc_scr, seed):
    r = SEQ_CHUNK
    first = (i % chunks_per_seq) == 0
    last = (i % chunks_per_seq) == chunks_per_seq - 1
    zero_halo = jnp.zeros((HALO, MIXER_WIDTH), F32)

    def fill(scr, prev, cur, nxt):
        scr[0:HALO, :] = jnp.where(first, zero_halo, prev[...])
        scr[HALO:HALO + r, :] = cur[...]
        scr[HALO + r:2 * HALO + r, :] = jnp.where(last, zero_halo, nxt[...])

    fill(pool_scr, up_prev, up_cur, up_next)
    fill(conv_scr, gl_prev, gl_cur, gl_next)

    row = lax.broadcasted_iota(jnp.int32, (r, 1), 0) + (i % chunks_per_seq) * r
    seq_len = chunks_per_seq * r
    outs = []
    for g, win in enumerate(POOL_WINDOWS):
        lanes = slice(g * POOL_GROUP_DIM, (g + 1) * POOL_GROUP_DIM)
        u = pool_scr[:, lanes]
        s = _shift_rows(u, 1) + u
        span = 2
        while span < win:
            s = _shift_rows(s, span // 2) + _shift_rows(s, -(span // 2))
            span *= 2
        lo = jnp.maximum(row - win // 2, 0)
        hi = jnp.minimum(row + win // 2 - 1, seq_len - 1)
        cnt = (hi - lo + 1).astype(F32)
        p = s[HALO:HALO + r] / cnt - u[HALO:HALO + r]
        outs.append(jnp.dot(p.astype(BF16), pw_ref[g], preferred_element_type=F32))
    y_pool = (jnp.concatenate(outs, axis=1) * ps_ref[...]).astype(BF16)

    shifted_rows = r + SUBLANES * (CONV_TAPS // SUBLANES)
    for lo in range(1, SUBLANES):
        shift_scr[lo - 1] = conv_scr[lo:lo + shifted_rows, :]
    groups = CONV_SUB // SUBLANES
    for c in range(r // CONV_SUB):
        acc = jnp.zeros((groups, SUBLANES, MIXER_WIDTH), F32)
        zero = seed(c)
        if zero is not None:
            acc = acc + jnp.concatenate([zero] * (MIXER_WIDTH // LANES), axis=1)
        for j in range(CONV_TAPS):
            hi, lo = divmod(j + 1, SUBLANES)
            src = conv_scr if lo == 0 else shift_scr.at[lo - 1]
            start = c * CONV_SUB + SUBLANES * hi
            rows = src[start:start + CONV_SUB, :].reshape(groups, SUBLANES, MIXER_WIDTH)
            acc = acc + rows * dw_ref[j]
        acc_scr[c * CONV_SUB:(c + 1) * CONV_SUB, :] = acc.reshape(CONV_SUB, MIXER_WIDTH)
    y = acc_scr[...] + dwb_ref[...]
    mu = jnp.mean(y, axis=-1, keepdims=True)
    yc = y - mu
    yn = yc * lax.rsqrt(jnp.mean(yc * yc, axis=-1, keepdims=True) + EPS) * lng_ref[...] + lnb_ref[...]
    act = yn * jax.nn.sigmoid(yn)
    y_conv = (jnp.dot(act.astype(BF16), cw_ref[...], preferred_element_type=F32)
              + cwb_ref[...]).astype(BF16)
    return y_pool, y_conv


def _outproj_kernel(x_ref, p0, p1, p2, p3, w_ref, g1_ref, sh_ref, sc_ref, gn_ref, x1_ref, h2_ref):
    half = x_ref.shape[0] // 2
    for rows in (slice(0, half), slice(half, 2 * half)):
        mix = jnp.concatenate([p0[rows, :], p1[rows, :], p2[rows, :], p3[rows, :]], axis=1)
        a = jnp.dot(mix, w_ref[...], preferred_element_type=F32)
        x1 = x_ref[rows, :] + g1_ref[...] * a
        x1_ref[rows, :] = x1
        h2_ref[rows, :] = (_rms(x1, gn_ref[...]) * (1.0 + sc_ref[...]) + sh_ref[...]).astype(BF16)


def _outproj(x, parts, w_out, mod, g_norm2, layer, row_of, tm, name):
    m_rows = x.shape[0]
    w = MIXER_WIDTH
    row_spec = lambda: pl.BlockSpec((tm, D_MODEL), lambda i: (i, 0))
    part_spec = lambda: pl.BlockSpec((tm, w), lambda i: (i, 0))
    return pl.pallas_call(
        _outproj_kernel,
        grid=(m_rows // tm,),
        in_specs=[row_spec(), part_spec(), part_spec(), part_spec(), part_spec(),
                  pl.BlockSpec((None, D_MODEL, D_MODEL), lambda i: (0, 0, 0), pipeline_mode=pl.Buffered(1)),
                  _mod_spec(layer, 2, row_of), _mod_spec(layer, 3, row_of), _mod_spec(layer, 4, row_of),
                  pl.BlockSpec((None, 1, D_MODEL), lambda i: (layer, 0, 0))],
        out_specs=[row_spec(), row_spec()],
        out_shape=[jax.ShapeDtypeStruct((m_rows, D_MODEL), F32),
                   jax.ShapeDtypeStruct((m_rows, D_MODEL), BF16)],
        compiler_params=_params("arbitrary"),
        name=name,
    )(x, *parts, w_out, mod, mod, mod, g_norm2.reshape(DEPTH, 1, D_MODEL))


def _fourier_kernel(seq_len, scale, u_ref, pos_ref, ch_ref, fw_ref, o_ref, ab_scr, pos_scr):
    b = pl.program_id(1)

    @pl.when(b == 0)
    def _():
        pos_scr[...] = pos_ref[...].astype(BF16)

    @pl.when(pl.program_id(0) == 0)
    def _():
        ch = ch_ref[...].astype(BF16)
        for h in range(MIXER_WIDTH // FOURIER_HEAD_DIM):
            lanes = slice(h * FOURIER_HEAD_DIM, (h + 1) * FOURIER_HEAD_DIM)
            a = jnp.dot(u_ref[:, lanes], ch, preferred_element_type=F32)
            ab_scr[b, 0:seq_len, lanes] = a[:, :FOURIER_HEAD_DIM].astype(BF16)
            ab_scr[b, seq_len:2 * seq_len, lanes] = a[:, FOURIER_HEAD_DIM:].astype(BF16)

    f = jnp.dot(pos_scr[...], ab_scr[b], preferred_element_type=F32) * scale
    o_ref[...] = jnp.dot(f.astype(BF16), fw_ref[...], preferred_element_type=F32).astype(BF16)


def _fourier(ufour, fourier_w, layer, seq_len, name):
    m_rows = ufour.shape[0]
    w = MIXER_WIDTH
    batch = m_rows // seq_len
    tl = min(seq_len, 512)
    blocks = seq_len // tl
    pos_tab, ch_tab = _dft_tables(seq_len)
    scale = float(1.0 / np.sqrt(seq_len * FOURIER_HEAD_DIM))
    return pl.pallas_call(
        functools.partial(_fourier_kernel, seq_len, scale),
        grid=(blocks, batch),
        in_specs=[pl.BlockSpec((seq_len, w), lambda j, b: (jnp.where(j == 0, b, batch - 1), 0)),
                  pl.BlockSpec((tl, 2 * seq_len), lambda j, b: (j, 0)),
                  pl.BlockSpec((FOURIER_HEAD_DIM, 2 * FOURIER_HEAD_DIM), lambda j, b: (0, 0)),
                  pl.BlockSpec((None, w, w), lambda j, b: (layer, 0, 0))],
        out_specs=pl.BlockSpec((tl, w), lambda j, b: (b * blocks + j, 0)),
        out_shape=jax.ShapeDtypeStruct((m_rows, w), BF16),
        scratch_shapes=[pltpu.VMEM((batch, 2 * seq_len, w), BF16), pltpu.VMEM((tl, 2 * seq_len), BF16)],
        compiler_params=_params("arbitrary", "arbitrary"),
        name=name,
    )(ufour, pos_tab, ch_tab, fourier_w)


def _cast_blocks(refs):
    half = len(refs) // 2
    for src, dst in zip(refs[:half], refs[half:]):
        if len(dst.shape) == len(src.shape):
            dst[...] = src[...].astype(BF16)
        else:
            for j in range(dst.shape[0]):
                dst[j] = src[:, j * FFN_TF:(j + 1) * FFN_TF].astype(BF16)


def _ffn_weight_cast_specs(weights, layer, n_row, n_col, row_col_of):
    br, bc = D_MODEL // n_row, D_FF // n_col
    slabs = bc // FFN_TF

    def up_map(*ids):
        r, c = row_col_of(*ids)
        return (layer, r, c)

    def down_map(*ids):
        r, c = row_col_of(*ids)
        return (layer, c, r)

    def tiled_map(*ids):
        r, c = row_col_of(*ids)
        return (0, c, r, 0)

    in_specs = [pl.BlockSpec((None, br, bc), up_map), pl.BlockSpec((None, br, bc), up_map),
                pl.BlockSpec((None, bc, br), down_map)]
    out_specs = [pl.BlockSpec((None, slabs, br, FFN_TF), tiled_map),
                 pl.BlockSpec((None, slabs, br, FFN_TF), tiled_map),
                 pl.BlockSpec((None, bc, br), lambda *ids: (0,) + down_map(*ids)[1:])]
    tiled = jax.ShapeDtypeStruct((1, D_FF // FFN_TF, D_MODEL, FFN_TF), BF16)
    out_shape = [tiled, tiled, jax.ShapeDtypeStruct((1, D_FF, D_MODEL), BF16)]
    return in_specs, out_specs, out_shape


def _attn_kernel(lam_init, cached, heads, n_cast, *refs):
    n_in = (7 if cached else 5) + n_cast
    cast_src, cast_dst = refs[n_in - n_cast:n_in], refs[n_in + 1:n_in + 1 + n_cast]
    o_ref = refs[n_in]
    scratch = refs[n_in + 1 + n_cast:]
    if cached:
        q_ref, k_ref, v_ref, ck_ref, cv_ref, lam_ref, gs_ref = refs[:7]
        vt_scr, s_scr, ckb_scr = scratch
    else:
        q_ref, k_ref, v_ref, lam_ref, gs_ref = refs[:5]
        vt_scr, s_scr = scratch
    _cast_blocks(cast_src + cast_dst)
    tq = q_ref.shape[0]
    own = k_ref.shape[0]
    past = ck_ref.shape[0] if cached else 0

    @pl.when(pl.program_id(2) == 0)
    def _():
        for h in range(heads):
            cols = slice(h * V_DIM, (h + 1) * V_DIM)
            base = h * VT_ROWS
            vt_scr[base:base + V_DIM, 0:own] = v_ref[:, cols].astype(F32).T.astype(BF16)
            vt_scr[base + V_DIM:base + VT_ROWS, :] = jnp.ones((VT_ROWS - V_DIM, own + past), BF16)
            if cached:
                vt_scr[base:base + V_DIM, own:] = cv_ref[:, cols].T.astype(BF16)
                ckb_scr[:, cols] = ck_ref[:, cols].astype(BF16)

    lv = lam_ref[...]
    lam = (jnp.exp(jnp.sum(lv[0:1] * lv[1:2], axis=-1, keepdims=True))
           - jnp.exp(jnp.sum(lv[2:3] * lv[3:4], axis=-1, keepdims=True)) + lam_init)
    lane = lax.broadcasted_iota(jnp.int32, (1, V_DIM), 1)
    nt = (((1,), (1,)), ((), ()))
    segments = [(k_ref, 0, own)] + ([(ckb_scr, own, past)] if cached else [])
    items = [(h, key_ref, offset, c0, min(KEY_CHUNK, count - c0))
             for h in range(heads) for key_ref, offset, count in segments
             for c0 in range(0, count, KEY_CHUNK)]
    qqs = []
    for h in range(heads):
        q = q_ref[:, h * V_DIM:(h + 1) * V_DIM]
        zero = jnp.zeros_like(q)
        qqs.append(jnp.concatenate([jnp.where(lane < QK_DIM, q, zero),
                                    jnp.where(lane >= QK_DIM, q, zero)], axis=0))

    def scores(n):
        h, key_ref, _, c0, kc = items[n]
        s_scr[n % 2, 0:kc, :] = lax.dot_general(key_ref[c0:c0 + kc, h * V_DIM:(h + 1) * V_DIM], qqs[h], nt,
                                                preferred_element_type=F32)

    scores(0)
    parts = [[] for _ in range(heads)]
    for n, (h, _, offset, c0, kc) in enumerate(items):
        if n + 1 < len(items):
            scores(n + 1)
        s = s_scr[n % 2, 0:kc, :]
        mc = s.max(axis=0, keepdims=True)
        e = jnp.exp2(s - mc).astype(BF16)
        oc = jnp.dot(vt_scr[h * VT_ROWS:(h + 1) * VT_ROWS, offset + c0:offset + c0 + kc], e,
                     preferred_element_type=F32)
        parts[h].append((mc, oc))
    for h in range(heads):
        m = functools.reduce(jnp.maximum, [mc for mc, _ in parts[h]])
        acc = functools.reduce(jnp.add, [oc * jnp.exp2(mc - m) for mc, oc in parts[h]])
        inv = 1.0 / acc[V_DIM:V_DIM + 1]
        o_t = acc[:V_DIM, :tq] * inv[:, :tq] - lam * (acc[:V_DIM, tq:] * inv[:, tq:])
        o_ref[:, h * V_DIM:(h + 1) * V_DIM] = (_rms(o_t.T, gs_ref[...]) * (1.0 - lam_init)).astype(BF16)


def _attention(q, k, v, cache_k, cache_v, lam, g_subln, layer, lam_init, seq_len, tq, heads, cast, name):
    m_rows = q.shape[0]
    batch = m_rows // seq_len
    qb = seq_len // tq
    cached = cache_k is not None
    hw = heads * V_DIM
    in_specs = [pl.BlockSpec((tq, hw), lambda b, h, i: (b * qb + i, h)),
                pl.BlockSpec((seq_len, hw), lambda b, h, i: (b, h)),
                pl.BlockSpec((seq_len, hw), lambda b, h, i: (b, h))]
    args = [q, k, v]
    past = 0
    scratch = []
    if cached:
        past = cache_k.shape[2]
        spec = lambda: pl.BlockSpec((None, None, past, hw), lambda b, h, i: (b, layer, 0, h))
        in_specs += [spec(), spec()]
        args += [cache_k, cache_v]
        scratch = [pltpu.VMEM((past, hw), BF16)]
    in_specs += [pl.BlockSpec((None, 4, QK_DIM), lambda b, h, i: (layer, 0, 0)),
                 pl.BlockSpec((None, 1, V_DIM), lambda b, h, i: (layer, 0, 0))]
    args += [lam, g_subln.reshape(DEPTH, 1, V_DIM)]
    groups = N_HEADS // heads
    out_specs = [pl.BlockSpec((tq, hw), lambda b, h, i: (b * qb + i, h))]
    out_shape = [jax.ShapeDtypeStruct((m_rows, N_HEADS * V_DIM), BF16)]
    n_cast = 0
    if cast is not None:
        weights, cast_layer = cast
        assert groups == 1
        ci, co, cs = _ffn_weight_cast_specs(weights, cast_layer, batch * qb, 1,
                                            lambda b, h, i: (b * qb + i, 0))
        in_specs += ci
        out_specs += co
        out_shape += cs
        args += list(weights)
        n_cast = len(weights)
    outs = pl.pallas_call(
        functools.partial(_attn_kernel, lam_init, cached, heads, n_cast),
        grid=(batch, groups, qb),
        in_specs=in_specs,
        out_specs=out_specs,
        out_shape=out_shape,
        scratch_shapes=[pltpu.VMEM((heads * VT_ROWS, seq_len + past), BF16),
                        pltpu.VMEM((2, min(KEY_CHUNK, seq_len), 2 * tq), F32)] + scratch,
        compiler_params=_params("arbitrary", "arbitrary", "arbitrary"),
        name=name,
    )(*args)
    return outs[0], tuple(outs[1:])


def _ffn_kernel(h_ref, x_ref, wg_ref, wu_ref, wd_ref, g2_ref, o_ref, acc_scr):
    f = pl.program_id(1)
    last = pl.num_programs(1) - 1

    def partial_down():
        h = h_ref[...]
        g = jnp.dot(h, wg_ref[...], preferred_element_type=F32)
        u = jnp.dot(h, wu_ref[...], preferred_element_type=F32)
        a = (g * jax.nn.sigmoid(g) * u).astype(BF16)
        return jnp.dot(a, wd_ref[...], preferred_element_type=F32)

    @pl.when(f == 0)
    def _():
        acc_scr[...] = partial_down()

    @pl.when((f > 0) & (f < last))
    def _():
        acc_scr[...] += partial_down()

    @pl.when(f == last)
    def _():
        o_ref[...] = x_ref[...] + g2_ref[...] * (acc_scr[...] + partial_down())


def _ffn(h2, x1, ffn_w, mod, layer, row_of, tm, name):
    tf = FFN_TF
    m_rows = x1.shape[0]
    return pl.pallas_call(
        _ffn_kernel,
        grid=(m_rows // tm, D_FF // tf),
        in_specs=[pl.BlockSpec((tm, D_MODEL), lambda i, f: (i, 0)),
                  pl.BlockSpec((tm, D_MODEL), lambda i, f: (i, 0)),
                  pl.BlockSpec((None, None, D_MODEL, tf), lambda i, f: (0, f, 0, 0)),
                  pl.BlockSpec((None, None, D_MODEL, tf), lambda i, f: (0, f, 0, 0)),
                  pl.BlockSpec((None, tf, D_MODEL), lambda i, f: (0, f, 0)),
                  _mod_spec(layer, 5, row_of)],
        out_specs=pl.BlockSpec((tm, D_MODEL), lambda i, f: (i, 0)),
        out_shape=jax.ShapeDtypeStruct((m_rows, D_MODEL), F32),
        scratch_shapes=[pltpu.VMEM((tm, D_MODEL), F32)],
        compiler_params=_params("arbitrary", "arbitrary"),
        name=name,
    )(h2, x1, *ffn_w, mod)


def kernel(x_prompt, x_sample, cache_k, cache_v, c, c_ctx, w_ada, b_ada, g_norm1, w_in, pool_w, pool_scale, g_q, g_k, lam, g_subln, conv_dw, conv_dw_b, conv_ln_g, conv_ln_b, conv_pw, conv_pw_b, fourier_w, w_out, g_norm2, w_gate, w_up, w_down):
    batch, seq, _ = x_prompt.shape
    dec_batch, dec_seq, _ = x_sample.shape
    past = cache_k.shape[2]
    tm = 512

    cc = jnp.concatenate([c_ctx[None, :], c, jnp.zeros((8 - 1 - dec_batch, D_MODEL), F32)], axis=0)
    mod = _modulation(cc, w_ada, b_ada).reshape(DEPTH, 8, 1, 6 * D_MODEL)

    bf = lambda a: a.astype(BF16)
    fourier_w_b = bf(fourier_w)
    mix_params = dict(pool_w=bf(pool_w), pool_scale=pool_scale, conv_dw=conv_dw, conv_dw_b=conv_dw_b,
                      conv_ln_g=conv_ln_g, conv_ln_b=conv_ln_b, conv_pw=bf(conv_pw), conv_pw_b=conv_pw_b)
    ffn_f32 = (w_gate, w_up, w_down)
    ck = cache_k.reshape(dec_batch, DEPTH, past, N_HEADS * V_DIM)
    cv = cache_v.reshape(dec_batch, DEPTH, past, N_HEADS * V_DIM)

    streams = {
        "lat": dict(x=x_sample.reshape(dec_batch * dec_seq, D_MODEL), seq=dec_seq, latent=True,
                    row_of=lambda i: 1 + i // (dec_seq // tm)),
        "ctx": dict(x=x_prompt.reshape(batch * seq, D_MODEL), seq=seq, latent=False,
                    row_of=lambda i: 0),
    }
    cache_kv = None
    ffn_b = None
    proj_b = (bf(w_in[0:1]), bf(w_out[0:1]))
    proj_next = None
    for layer in range(DEPTH):
        lam_init = 0.8 - 0.6 * float(np.exp(-0.3 * layer))
        if layer > 0:
            proj_b = proj_next
        for name, st in streams.items():
            latent, seq_len = st["latent"], st["seq"]
            cast = ((w_in, w_out), layer + 1) if (latent and layer + 1 < DEPTH) else None
            outs = _inproj(st["x"], mod, g_norm1, proj_b[0], g_q, g_k, mix_params, layer, seq_len, latent, tm,
                           prev_kv=cache_kv if (not latent and layer > 0) else None, cast=cast)
            if cast is not None:
                proj_next = tuple(outs[-2:])
            y_pool, q, k, v, y_conv, ufour = outs[:6]
            if not latent:
                cache_kv = (outs[6], outs[7])
            y_four = _fourier(ufour, fourier_w_b, layer, seq_len, "fourier_" + name)
            att, cast_out = _attention(q, k, v, ck if latent else None, cv if latent else None, lam, g_subln,
                                       layer, lam_init, seq_len, 512 if latent else seq_len, N_HEADS,
                                       (ffn_f32, layer) if latent else None, "attn_" + name)
            if cast_out:
                ffn_b = cast_out
            x1, h2 = _outproj(st["x"], (y_pool, att, y_conv, y_four), proj_b[1], mod, g_norm2, layer,
                              st["row_of"], tm, "outproj_" + name)
            st["x"] = _ffn(h2, x1, ffn_b, mod, layer, st["row_of"], tm, "ffn_" + name)
    y_prompt = streams["ctx"]["x"].reshape(batch, seq, D_MODEL)
    y_sample = streams["lat"]["x"].reshape(dec_batch, dec_seq, D_MODEL)
    new_k = cache_kv[0].reshape(batch, DEPTH, seq, N_HEADS, 2, QK_DIM)
    new_v = cache_kv[1].reshape(batch, DEPTH, seq, N_HEADS, V_DIM)
    return (y_prompt, y_sample, new_k, new_v)
```
